```python
import math
import jax
import jax.numpy as jnp
from jax import lax
import numpy as np

D_MODEL = 1024
BATCH = 8
SEQ = 2048
DEPTH = 4
DEC_BATCH = 128
DEC_SEQ = 8
PAST_LEN = 2048
PAGE_SIZE = 128

HEAD_DIM = 64
A_HEADS = 4
B_HEADS = 6
C_HEADS = 6
C_KV_HEADS = 2
C_GROUP = C_HEADS // C_KV_HEADS
MIX_WIDTH = (A_HEADS + B_HEADS + C_HEADS) * HEAD_DIM
MOBA_BLOCK = 256
MOBA_TOPK = 3
CMP_BLOCK = 32
CMP_STRIDE = 16
CMP_HIDDEN = 128
SLC_BLOCK = 64
SLC_TOPK = 8
WINDOW = 512
N_GATES = 3
N_BUCKETS = 32
MAX_DISTANCE = 128
D_FF = 2816
CONV_W = 3
FOX_QUERY_BLOCK = 128
MOBA_QUERY_BLOCK = 32
NSA_QUERY_BLOCK = 64
RMS_EPS = 1e-6
NEG_INF = -1e30
FORCE_SCORE = 1e6
POOL_NUM = 5
POOL_DEN = 4
SPLIT_SIZES = (A_HEADS * HEAD_DIM, A_HEADS * HEAD_DIM, A_HEADS * HEAD_DIM,
               B_HEADS * HEAD_DIM, B_HEADS * HEAD_DIM, B_HEADS * HEAD_DIM, B_HEADS,
               C_HEADS * HEAD_DIM,
               C_KV_HEADS * HEAD_DIM, C_KV_HEADS * HEAD_DIM, C_KV_HEADS * HEAD_DIM,
               C_KV_HEADS * HEAD_DIM, C_KV_HEADS * HEAD_DIM, C_KV_HEADS * HEAD_DIM,
               C_HEADS * N_GATES)
N_IN = sum(SPLIT_SIZES)

kernel_name = 'hybrid_moba_fox_nsa_decoder_step'


def _rms_norm(x, g):
    xf = x.astype(jnp.float32)
    y = xf * lax.rsqrt(jnp.mean(xf * xf, axis=-1, keepdims=True) + RMS_EPS)
    return (y * g.astype(jnp.float32)).astype(x.dtype)


def _rel_bucket(dist):
    n = jnp.maximum(dist, 0)
    exact = N_BUCKETS // 2
    nf = jnp.maximum(n, 1).astype(jnp.float32)
    large = exact + (jnp.log(nf / exact) / math.log(MAX_DISTANCE / exact)
                     * (N_BUCKETS - exact)).astype(jnp.int32)
    large = jnp.minimum(large, N_BUCKETS - 1)
    return jnp.where(n < exact, n, large)


def _masked_softmax(logits, mask):
    return jax.nn.softmax(jnp.where(mask, logits, NEG_INF), axis=-1)


def _sweep(fn, n_q, blk):
    if n_q <= blk or n_q % blk:
        return fn(0, n_q)
    out = lax.map(lambda i: fn(i * blk, blk), jnp.arange(n_q // blk))
    return jnp.moveaxis(out, 0, 1).reshape((out.shape[1], n_q) + out.shape[3:])


def _moba_mixer(q, qpos, k, v, tab, blk):
    B, L, H, D = k.shape
    nb = -(-L // MOBA_BLOCK)
    padw = ((0, 0), (0, nb * MOBA_BLOCK - L), (0, 0), (0, 0))
    to_blocks = lambda a: jnp.pad(a, padw).reshape(B, nb, MOBA_BLOCK, H, D).transpose(0, 3, 1, 2, 4)
    kb, vb = to_blocks(k), to_blocks(v)
    kmean = jnp.mean(kb.astype(jnp.float32), axis=3)
    tab_h = tab.T
    n_sel = min(MOBA_TOPK, nb)
    bi = jnp.arange(B)[:, None, None, None]
    hi = jnp.arange(H)[None, :, None, None]
    offs = jnp.arange(MOBA_BLOCK, dtype=jnp.int32)
    scale = HEAD_DIM ** -0.5

    def fn(start, size):
        qs = lax.dynamic_slice_in_dim(q, start, size, 1).transpose(0, 2, 1, 3)
        qp = lax.dynamic_slice_in_dim(qpos, start, size, 0)
        own = qp // MOBA_BLOCK
        gate = jnp.einsum('bhqd,bhnd->bhqn', qs.astype(jnp.float32), kmean)
        gate = jnp.where(jnp.arange(nb)[None, :] < own[:, None], gate, NEG_INF)
        _, idx = lax.top_k(gate, n_sel)
        own_b = jnp.broadcast_to(own[None, None, :, None], (B, H, size, 1)).astype(idx.dtype)
        ok = jnp.concatenate([idx < own[None, None, :, None],
                              jnp.ones_like(own_b, dtype=bool)], axis=-1)
        idx = jnp.concatenate([idx, own_b], axis=-1)
        kg, vg = kb[bi, hi, idx], vb[bi, hi, idx]
        kpos = idx[..., None] * MOBA_BLOCK + offs
        dist = qp[None, None, :, None, None] - kpos
        logits = (jnp.einsum('bhqd,bhqjsd->bhqjs', qs, kg).astype(jnp.float32) * scale
                  + tab_h[hi[..., None], _rel_bucket(dist)])
        mask = ok[..., None] & (dist >= 0)
        shp = logits.shape
        p = _masked_softmax(logits.reshape(shp[:3] + (-1,)), mask.reshape(shp[:3] + (-1,))).reshape(shp)
        return jnp.einsum('bhqjs,bhqjsd->bqhd', p.astype(vg.dtype), vg)

    return _sweep(fn, q.shape[1], blk)


def _fox_attend(q, qpos, cq, k, v, ck, kpos):
    logits = jnp.einsum('bqhd,bkhd->bhqk', q, k).astype(jnp.float32) * HEAD_DIM ** -0.5
    decay = jnp.swapaxes(cq, 1, 2)[..., :, None] - jnp.swapaxes(ck, 1, 2)[..., None, :]
    mask = kpos[None, :] <= qpos[:, None]
    p = _masked_softmax(logits + decay, mask)
    return jnp.einsum('bhqk,bkhd->bqhd', p.astype(v.dtype), v)


def _fox_mixer(q, qpos, k, v, logf, pos0, blk):
    c = jnp.cumsum(logf.astype(jnp.float32), axis=1)
    cq = c[:, pos0:]
    kpos = jnp.arange(k.shape[1], dtype=jnp.int32)

    def fn(start, size):
        sl = lambda a, ax: lax.dynamic_slice_in_dim(a, start, size, ax)
        return _fox_attend(sl(q, 1), sl(qpos, 0), sl(cq, 1), k, v, c, kpos)

    return _sweep(fn, q.shape[1], blk)


def _compress(blk, pe, w1, w2):
    hdn = jax.nn.gelu(jnp.einsum('bnlhd,ldf->bnhf', blk + pe[None, None, :, None, :], w1))
    return jnp.einsum('bnhf,fd->bnhd', hdn, w2)


def _nsa_mixer(q, qpos, ckv, gates, win_keys, pe, w1, w2, tab, blk):
    B, L = ckv.shape[:2]
    T = q.shape[1]
    scale = HEAD_DIM ** -0.5
    nc = (L - CMP_BLOCK) // CMP_STRIDE + 1
    cstart = np.arange(nc) * CMP_STRIDE
    cidx = cstart[:, None] + np.arange(CMP_BLOCK)[None, :]
    kc = _compress(ckv[:, :, 0][:, cidx], pe[0], w1[0], w2[0])
    vc = _compress(ckv[:, :, 1][:, cidx], pe[1], w1[1], w2[1])
    cend = jnp.asarray(cstart + CMP_BLOCK - 1, dtype=jnp.int32)
    nbs = -(-L // SLC_BLOCK)
    padw = ((0, 0), (0, nbs * SLC_BLOCK - L), (0, 0), (0, 0))
    to_blocks = lambda a: jnp.pad(a, padw).reshape(B, nbs, SLC_BLOCK, C_KV_HEADS, HEAD_DIM).transpose(0, 3, 1, 2, 4)
    ksb, vsb = to_blocks(ckv[:, :, 2]), to_blocks(ckv[:, :, 3])
    bstart = np.arange(nbs) * SLC_BLOCK
    cover = jnp.asarray((cstart[:, None] < bstart[None, :] + SLC_BLOCK)
                        & (cstart[:, None] + CMP_BLOCK > bstart[None, :]), dtype=jnp.float32)
    tab3 = tab.T.reshape(C_KV_HEADS, C_GROUP, N_BUCKETS)
    n_sel = min(SLC_TOPK, nbs)
    bi = jnp.arange(B)[:, None, None, None]
    hi = jnp.arange(C_KV_HEADS)[None, :, None, None]
    hi6 = jnp.arange(C_KV_HEADS).reshape(1, C_KV_HEADS, 1, 1, 1, 1)
    gi6 = jnp.arange(C_GROUP).reshape(1, 1, C_GROUP, 1, 1, 1)
    jj = jnp.arange(nbs)
    soffs = jnp.arange(SLC_BLOCK, dtype=jnp.int32)

    def fn(start, size):
        qs = lax.dynamic_slice_in_dim(q, start, size, 1).reshape(
            B, size, C_KV_HEADS, C_GROUP, HEAD_DIM).transpose(0, 2, 3, 1, 4)
        qp = lax.dynamic_slice_in_dim(qpos, start, size, 0)
        gs = jax.nn.sigmoid(lax.dynamic_slice_in_dim(gates, start, size, 1).astype(jnp.float32))
        gs = gs.reshape(B, size, C_KV_HEADS, C_GROUP, N_GATES).transpose(0, 2, 3, 1, 4)
        cdist = qp[:, None] - cend[None, :]
        cok = cdist >= 0
        cl = (jnp.einsum('bhgqd,bnhd->bhgqn', qs, kc).astype(jnp.float32) * scale
              + tab3[:, :, _rel_bucket(cdist)])
        p_cmp = _masked_softmax(cl, cok) * cok
        o_cmp = jnp.einsum('bhgqn,bnhd->bhgqd', p_cmp.astype(vc.dtype), vc)
        imp = jnp.einsum('bhgqn,nj->bhqj', p_cmp, cover)
        own = (qp // SLC_BLOCK)[:, None]
        forced = (jj == 0) | (jj == own) | (jj == own - 1)
        imp = jnp.where(jj <= own, jnp.where(forced, FORCE_SCORE, imp), NEG_INF)
        _, sidx = lax.top_k(imp, n_sel)
        sok = sidx <= own[None, None]
        kg, vg = ksb[bi, hi, sidx], vsb[bi, hi, sidx]
        kpos = sidx[..., None] * SLC_BLOCK + soffs
        sdist = qp[None, None, :, None, None] - kpos
        sl = (jnp.einsum('bhgqd,bhqksd->bhgqks', qs, kg).astype(jnp.float32) * scale
              + tab3[hi6, gi6, _rel_bucket(sdist)[:, :, None]])
        shp = sl.shape
        smask = jnp.broadcast_to((sok[..., None] & (sdist >= 0))[:, :, None], shp)
        p_slc = _masked_softmax(sl.reshape(shp[:4] + (-1,)), smask.reshape(shp[:4] + (-1,))).reshape(shp)
        o_slc = jnp.einsum('bhgqks,bhqksd->bhgqd', p_slc.astype(vg.dtype), vg)
        kw, vw, kwp = win_keys(start, size)
        wdist = qp[:, None] - kwp[None, :]
        wok = (wdist >= 0) & (wdist < WINDOW) & (kwp[None, :] >= 0)
        wl = (jnp.einsum('bhgqd,bwhd->bhgqw', qs, kw).astype(jnp.float32) * scale
              + tab3[:, :, _rel_bucket(wdist)])
        o_win = jnp.einsum('bhgqw,bwhd->bhgqd', _masked_softmax(wl, wok).astype(vw.dtype), vw)
        o = gs[..., 0:1] * o_cmp + gs[..., 1:2] * o_slc + gs[..., 2:3] * o_win
        return o.transpose(0, 3, 1, 2, 4).reshape(B, size, C_HEADS, HEAD_DIM).astype(q.dtype)

    return _sweep(fn, T, blk)


def _layer(x, pos0, past, lp, rel_tab):
    B, T, _ = x.shape
    qpos = pos0 + jnp.arange(T, dtype=jnp.int32)
    h = _rms_norm(x, lp['ln1_g'])
    parts = jnp.split(h @ lp['w_in'], np.cumsum(SPLIT_SIZES)[:-1].tolist(), axis=-1)
    aq, ak, av, bq, bk, bv, bf, cq, cck, ccv, csk, csv, cwk, cwv, cg = parts
    heads = lambda a: a.reshape(B, T, -1, HEAD_DIM)
    a_kv = jnp.stack([heads(ak), heads(av)], axis=2)
    b_kv = jnp.stack([heads(bk), heads(bv)], axis=2)
    b_logf = jax.nn.log_sigmoid((bf + lp['b_f']).astype(jnp.float32)).astype(x.dtype)
    c_kv = jnp.stack([heads(cck), heads(ccv), heads(csk), heads(csv)], axis=2)
    c_win = jnp.stack([heads(cwk), heads(cwv)], axis=2)
    if past is None:
        a_all, b_all, logf_all, c_all = a_kv, b_kv, b_logf, c_kv
        win_pad = jnp.pad(c_win, ((0, 0), (WINDOW, 0), (0, 0), (0, 0), (0, 0)))

        def win_keys(start, size):
            w = lax.dynamic_slice_in_dim(win_pad, start, size + WINDOW, 1)
            return w[:, :, 0], w[:, :, 1], start - WINDOW + jnp.arange(size + WINDOW, dtype=jnp.int32)

        win_state = c_win[:, -min(WINDOW, T):]
        g_prev = jnp.zeros((B, CONV_W - 1, D_FF), x.dtype)
        blocks = (MOBA_QUERY_BLOCK, FOX_QUERY_BLOCK, NSA_QUERY_BLOCK)
    else:
        a_all = jnp.concatenate([past['a_kv'], a_kv], axis=1)
        b_all = jnp.concatenate([past['b_kv'], b_kv], axis=1)
        logf_all = jnp.concatenate([past['b_logf'], b_logf], axis=1)
        c_all = jnp.concatenate([past['c_kv'], c_kv], axis=1)
        win_all = jnp.concatenate([past['c_win'], c_win], axis=1)
        wb = past['c_win'].shape[1]
        win_pos = pos0 - wb + jnp.arange(wb + T, dtype=jnp.int32)

        def win_keys(start, size):
            return win_all[:, :, 0], win_all[:, :, 1], win_pos

        win_state = win_all[:, -wb:]
        g_prev = past['conv']
        blocks = (1, FOX_QUERY_BLOCK, T)
    y_a = _moba_mixer(heads(aq), qpos, a_all[:, :, 0], a_all[:, :, 1], rel_tab[:, :A_HEADS], blocks[0])
    y_b = _fox_mixer(heads(bq), qpos, b_all[:, :, 0], b_all[:, :, 1], logf_all, pos0, blocks[1])
    y_c = _nsa_mixer(heads(cq), qpos, c_all, cg + lp['b_gate'], win_keys, lp['cmp_pe'],
                     lp['cmp_w1'], lp['cmp_w2'], rel_tab[:, A_HEADS:], blocks[2])
    y = jnp.concatenate([y_a, y_b, y_c], axis=2).reshape(B, T, MIX_WIDTH)
    x = x + y @ lp['w_out']
    h2 = _rms_norm(x, lp['ln2_g'])
    g_hist = jnp.concatenate([g_prev, h2 @ lp['w_gate']], axis=1)
    conv = lp['conv_b']
    for j in range(CONV_W):
        conv = conv + lp['conv_w'][j] * g_hist[:, j:j + T]
    x = x + (jax.nn.gelu(conv) * (h2 @ lp['w_up'])) @ lp['w_down']
    conv_state = g_hist[:, -(CONV_W - 1):]
    return x, (a_kv, b_kv, b_logf, c_kv, win_state, conv_state)


def setup_inputs(seed: int = 0) -> dict:
    key = jax.random.key(seed)
    ks = jax.random.split(key, 32)
    f32 = jnp.float32
    n_pages = PAST_LEN // PAGE_SIZE
    n_used = DEC_BATCH * n_pages
    n_pool = (POOL_NUM * n_used + POOL_DEN - 1) // POOL_DEN
    win_buf = min(WINDOW, PAST_LEN)
    nrm = lambda k, shape, s: jax.random.normal(k, shape, f32) * s
    page_table = jax.random.permutation(ks[0], n_pool)[:n_used].reshape(DEC_BATCH, n_pages).astype(jnp.int32)
    return {
        'x_prompt': nrm(ks[1], (BATCH, SEQ, D_MODEL), 1.0),
        'x_sample': nrm(ks[2], (DEC_BATCH, DEC_SEQ, D_MODEL), 1.0),
        'cache_a_kv': nrm(ks[3], (DEPTH, n_pool, PAGE_SIZE, 2, A_HEADS, HEAD_DIM), 1.0),
        'cache_b_kv': nrm(ks[4], (DEPTH, n_pool, PAGE_SIZE, 2, B_HEADS, HEAD_DIM), 1.0),
        'cache_b_logf': jax.nn.log_sigmoid(3.0 + nrm(ks[5], (DEPTH, n_pool, PAGE_SIZE, B_HEADS), 1.0)),
        'cache_c_kv': nrm(ks[6], (DEPTH, n_pool, PAGE_SIZE, 4, C_KV_HEADS, HEAD_DIM), 1.0),
        'state_c_win': nrm(ks[7], (DEPTH, DEC_BATCH, win_buf, 2, C_KV_HEADS, HEAD_DIM), 1.0),
        'state_ffn_conv': nrm(ks[8], (DEPTH, DEC_BATCH, CONV_W - 1, D_FF), 1.0),
        'page_table': page_table,
        'rel_bias': nrm(ks[9], (N_BUCKETS, A_HEADS + C_HEADS), 0.2),
        'ln1_g': 1.0 + nrm(ks[10], (DEPTH, D_MODEL), 0.05),
        'w_in': nrm(ks[11], (DEPTH, D_MODEL, N_IN), D_MODEL ** -0.5),
        'b_f': 3.0 + nrm(ks[12], (DEPTH, B_HEADS), 0.1),
        'b_gate': nrm(ks[13], (DEPTH, C_HEADS * N_GATES), 0.1),
        'cmp_pe': nrm(ks[14], (DEPTH, 2, CMP_BLOCK, HEAD_DIM), 0.1),
        'cmp_w1': nrm(ks[15], (DEPTH, 2, CMP_BLOCK, HEAD_DIM, CMP_HIDDEN), (CMP_BLOCK * HEAD_DIM) ** -0.5),
        'cmp_w2': nrm(ks[16], (DEPTH, 2, CMP_HIDDEN, HEAD_DIM), CMP_HIDDEN ** -0.5),
        'w_out': nrm(ks[17], (DEPTH, MIX_WIDTH, D_MODEL), MIX_WIDTH ** -0.5),
        'ln2_g': 1.0 + nrm(ks[18], (DEPTH, D_MODEL), 0.05),
        'w_gate': nrm(ks[19], (DEPTH, D_MODEL, D_FF), D_MODEL ** -0.5),
        'w_up': nrm(ks[20], (DEPTH, D_MODEL, D_FF), D_MODEL ** -0.5),
        'conv_w': nrm(ks[21], (DEPTH, CONV_W, D_FF), CONV_W ** -0.5),
        'conv_b': nrm(ks[22], (DEPTH, D_FF), 0.02),
        'w_down': nrm(ks[23], (DEPTH, D_FF, D_MODEL), D_FF ** -0.5),
        'final_g': 1.0 + nrm(ks[24], (D_MODEL,), 0.05),
    }


def reference(x_prompt, x_sample, cache_a_kv, cache_b_kv, cache_b_logf, cache_c_kv,
              state_c_win, state_ffn_conv, page_table, rel_bias, ln1_g, w_in, b_f, b_gate,
              cmp_pe, cmp_w1, cmp_w2, w_out, ln2_g, w_gate, w_up, conv_w, conv_b, w_down, final_g):
    past_len = page_table.shape[1] * cache_a_kv.shape[2]

    def paged(c):
        g = c[page_table]
        return g.reshape((g.shape[0], past_len) + g.shape[3:])

    xp, xs = x_prompt, x_sample
    new_p, new_s = [], []
    for l in range(DEPTH):
        lp = {'ln1_g': ln1_g[l], 'w_in': w_in[l], 'b_f': b_f[l], 'b_gate': b_gate[l],
              'cmp_pe': cmp_pe[l], 'cmp_w1': cmp_w1[l], 'cmp_w2': cmp_w2[l], 'w_out': w_out[l],
              'ln2_g': ln2_g[l], 'w_gate': w_gate[l], 'w_up': w_up[l], 'conv_w': conv_w[l],
              'conv_b': conv_b[l], 'w_down': w_down[l]}
        xp, sp = _layer(xp, 0, None, lp, rel_bias)
        past = {'a_kv': paged(cache_a_kv[l]), 'b_kv': paged(cache_b_kv[l]),
                'b_logf': paged(cache_b_logf[l]), 'c_kv': paged(cache_c_kv[l]),
                'c_win': state_c_win[l], 'conv': state_ffn_conv[l]}
        xs, ss = _layer(xs, past_len, past, lp, rel_bias)
        new_p.append(sp)
        new_s.append(ss)
    y_prompt = _rms_norm(xp, final_g)
    y_sample = _rms_norm(xs, final_g)
    st = lambda rows, i: jnp.stack([r[i] for r in rows], axis=0)
    return (y_prompt, y_sample,
            st(new_p, 0), st(new_p, 1), st(new_p, 2), st(new_p, 3), st(new_p, 4), st(new_p, 5),
            st(new_s, 0), st(new_s, 1), st(new_s, 2), st(new_s, 3), st(new_s, 4), st(new_s, 5))
```

```python
import functools
import math

import numpy as np
import jax
import jax.numpy as jnp
from jax import lax
from jax.experimental import pallas as pl
from jax.experimental.pallas import tpu as pltpu

F32 = jnp.float32
BF16 = jnp.bfloat16
HIGHEST = lax.Precision.HIGHEST

D_MODEL = 1024
HEAD_DIM = 64
A_HEADS = 4
B_HEADS = 6
C_HEADS = 6
C_KV_HEADS = 2
C_GROUP = 3
MOBA_BLOCK = 256
MOBA_TOPK = 3
CMP_BLOCK = 32
CMP_STRIDE = 16
CMP_HIDDEN = 128
SLC_BLOCK = 64
SLC_TOPK = 8
WINDOW = 512
N_GATES = 3
N_BUCKETS = 32
MAX_DISTANCE = 128
D_FF = 2816
CONV_W = 3
RMS_EPS = 1e-6
NEG_INF = -1e30
FORCE_SCORE = 1e6
PAGE_SIZE = 128

LANES = 128
SUBLANES = 8
TQ = 256
VMEM_LIMIT = 56 * 1024 * 1024

C_SLOT_HEADS = (0, 3, 1, 4, 2, 5)

Q_COLS = (A_HEADS + B_HEADS + C_HEADS) * HEAD_DIM
AKV_COLS = 2 * A_HEADS * HEAD_DIM
BKV_COLS = 2 * B_HEADS * HEAD_DIM
CKV_COLS = 4 * C_KV_HEADS * HEAD_DIM
CWIN_COLS = 2 * C_KV_HEADS * HEAD_DIM
SMALL_COLS = LANES
IN_COLS = Q_COLS + AKV_COLS + BKV_COLS + CKV_COLS + CWIN_COLS + SMALL_COLS

_NT = (((1,), (1,)), ((), ()))


def _bucket_table():
    n = np.arange(MAX_DISTANCE + 1)
    exact = N_BUCKETS // 2
    nf = np.maximum(n, 1).astype(np.float32)
    large = exact + (np.log(nf / exact) / math.log(MAX_DISTANCE / exact) * (N_BUCKETS - exact)).astype(np.int32)
    large = np.minimum(large, N_BUCKETS - 1)
    return np.where(n < exact, n, large).astype(np.int32)


_BUCKETS = _bucket_table()


def _cparams(sem, vmem=VMEM_LIMIT):
    return pltpu.CompilerParams(dimension_semantics=sem, vmem_limit_bytes=vmem)


def _gelu(x):
    return 0.5 * x * (1.0 + jnp.tanh(math.sqrt(2.0 / math.pi) * (x + 0.044715 * (x * x * x))))


def _log_sigmoid(x):
    return -(jnp.maximum(-x, 0.0) + jnp.log1p(jnp.exp(-jnp.abs(x))))


def _in_proj_kernel(x_ref, g_ref, w_ref, sb_ref, q_ref, akv_ref, bkv_ref, ckv_ref, cwin_ref, small_ref):
    x = x_ref[...]
    ms = jnp.mean(x * x, axis=-1, keepdims=True)
    h = (x * lax.rsqrt(ms + RMS_EPS)) * g_ref[...]
    hb = h.astype(BF16)
    off = 0
    for ref in (q_ref, akv_ref, bkv_ref, ckv_ref, cwin_ref):
        n = ref.shape[-1]
        ref[...] = jnp.dot(hb, w_ref[:, off:off + n], preferred_element_type=F32)
        off += n
    raw = jnp.dot(hb, w_ref[:, off:off + SMALL_COLS], preferred_element_type=F32) + sb_ref[...]
    lane = lax.broadcasted_iota(jnp.int32, raw.shape, 1)
    small_ref[...] = jnp.where(lane < B_HEADS, _log_sigmoid(raw), raw)


def _in_proj(x2d, g, w, small_bias, tm=256):
    n = x2d.shape[0]
    tm = min(tm, n)
    widths = (Q_COLS, AKV_COLS, BKV_COLS, CKV_COLS, CWIN_COLS, SMALL_COLS)
    return pl.pallas_call(
        _in_proj_kernel,
        grid=(n // tm,),
        in_specs=[pl.BlockSpec((tm, D_MODEL), lambda i: (i, 0)),
                  pl.BlockSpec((1, D_MODEL), lambda i: (0, 0)),
                  pl.BlockSpec((D_MODEL, IN_COLS), lambda i: (0, 0)),
                  pl.BlockSpec((1, SMALL_COLS), lambda i: (0, 0))],
        out_specs=[pl.BlockSpec((tm, c), lambda i: (i, 0)) for c in widths],
        out_shape=[jax.ShapeDtypeStruct((n, c), F32) for c in widths],
        compiler_params=_cparams(("parallel",)),
        name="in_proj",
    )(x2d, g.reshape(1, D_MODEL), w, small_bias)


def _flash_init(s, vt):
    m = jnp.max(s, axis=0, keepdims=True)
    p = jnp.exp(s - m)
    l = jnp.sum(p, axis=0, keepdims=True)
    acc = jnp.dot(vt, p.astype(BF16), preferred_element_type=F32)
    return m, l, acc


def _flash_step(s, vt, m, l, acc):
    m_new = jnp.maximum(m, jnp.max(s, axis=0, keepdims=True))
    a = jnp.exp(m - m_new)
    p = jnp.exp(s - m_new)
    l = a * l + jnp.sum(p, axis=0, keepdims=True)
    acc = a * acc + jnp.dot(vt, p.astype(BF16), preferred_element_type=F32)
    return m_new, l, acc


def _top_rank(vals, row):
    rank = jnp.zeros(vals.shape, F32)
    for j in range(vals.shape[0]):
        vj = vals[j:j + 1, :]
        beats = (vj > vals) | ((vj == vals) & (j < row))
        rank = rank + beats.astype(F32)
    return rank


def _head_mask(q, h):
    lane = lax.broadcasted_iota(jnp.int32, (1, LANES), 1)
    return jnp.where((lane // HEAD_DIM) == h, q, 0.0)


def _moba_p_kernel(far_ref, q_ref, k_ref, v_ref, d_ref, o_ref, kb, vtb, kmean, sel):
    p = pl.program_id(1)
    qi = pl.program_id(2)
    nb = kmean.shape[0]

    @pl.when(qi == 0)
    def _prep():
        k = k_ref[0]
        kb[...] = k.astype(BF16)
        vtb[...] = v_ref[0].T.astype(BF16)
        for j in range(nb):
            kmean[j:j + 1, :] = jnp.mean(k[j * TQ:(j + 1) * TQ, :], axis=0, keepdims=True)

    q = q_ref[0]
    rowj = lax.broadcasted_iota(jnp.int32, (nb, TQ), 0)
    qbs = []
    for h in range(2):
        qh = _head_mask(q, h)
        qbs.append(qh.astype(BF16))
        g = lax.dot_general(kmean[...], qh, _NT, precision=HIGHEST, preferred_element_type=F32)
        gm = jnp.where(rowj < qi, g, NEG_INF)
        chosen = (_top_rank(gm, rowj) < MOBA_TOPK) & (rowj < qi)
        sel[h] = jnp.where(chosen, 0.0, NEG_INF)

    kk = lax.broadcasted_iota(jnp.int32, (TQ, TQ), 0)
    qq = lax.broadcasted_iota(jnp.int32, (TQ, TQ), 1)
    causal = kk <= qq

    def scores(h, j):
        kblk = kb[pl.ds(pl.multiple_of(j * TQ, TQ), TQ), :]
        return lax.dot_general(kblk, qbs[h], _NT, preferred_element_type=F32)

    def vt_of(h, j):
        return vtb[h * HEAD_DIM:(h + 1) * HEAD_DIM, pl.ds(pl.multiple_of(j * TQ, TQ), TQ)]

    state = []
    for h in range(2):
        s = jnp.where(causal, scores(h, qi) + d_ref[h, 0], NEG_INF)
        state.extend(_flash_init(s, vt_of(h, qi)))

    def body(j, st):
        out = []
        for h in range(2):
            bias = jnp.where(j == qi - 1, d_ref[h, 1], far_ref[2 * p + h])
            s = scores(h, j) + bias + sel[h, pl.ds(j, 1), :]
            out.extend(_flash_step(s, vt_of(h, j), *st[3 * h:3 * h + 3]))
        return tuple(out)

    st = lax.fori_loop(0, qi, body, tuple(state))
    o = jnp.concatenate([st[2] / st[1], st[5] / st[4]], axis=0)
    o_ref[0] = o.T


def _moba_prompt(q_all, a_kv, dtiles, far):
    b, t, _ = q_all.shape
    npair = A_HEADS // 2
    nb = t // TQ
    return pl.pallas_call(
        _moba_p_kernel,
        grid=(b, npair, nb),
        in_specs=[pl.BlockSpec(memory_space=pltpu.SMEM),
                  pl.BlockSpec((1, TQ, LANES), lambda bi, p, qi: (bi, qi, p)),
                  pl.BlockSpec((1, t, LANES), lambda bi, p, qi: (bi, 0, p)),
                  pl.BlockSpec((1, t, LANES), lambda bi, p, qi: (bi, 0, npair + p)),
                  pl.BlockSpec((2, 2, TQ, TQ), lambda bi, p, qi: (p, 0, 0, 0))],
        out_specs=pl.BlockSpec((1, TQ, LANES), lambda bi, p, qi: (bi, qi, p)),
        out_shape=jax.ShapeDtypeStruct((b, t, A_HEADS * HEAD_DIM), F32),
        scratch_shapes=[pltpu.VMEM((t, LANES), BF16), pltpu.VMEM((LANES, t), BF16),
                        pltpu.VMEM((nb, LANES), F32), pltpu.VMEM((2, nb, TQ), F32)],
        compiler_params=_cparams(("parallel", "parallel", "arbitrary")),
        name="moba_prompt",
    )(far, q_all, a_kv, a_kv, dtiles)


def _fox_p_kernel(q_ref, k_ref, v_ref, lf_ref, o_ref, kb, vtb, cum, ckb):
    p = pl.program_id(1)
    qi = pl.program_id(2)
    t = kb.shape[0]
    nb = t // TQ

    @pl.when(qi == 0)
    def _prep():
        kb[...] = k_ref[0].astype(BF16)
        vtb[...] = v_ref[0].T.astype(BF16)
        r = lax.broadcasted_iota(jnp.int32, (TQ, TQ), 0)
        c = lax.broadcasted_iota(jnp.int32, (TQ, TQ), 1)
        tri = (c <= r).astype(F32)
        carry = jnp.zeros((1, LANES), F32)
        for j in range(nb):
            blk = jnp.dot(tri, lf_ref[0, j * TQ:(j + 1) * TQ, :], precision=HIGHEST,
                          preferred_element_type=F32) + carry
            cum[j * TQ:(j + 1) * TQ, :] = blk
            carry = blk[TQ - 1:TQ, :]
        sr = lax.broadcasted_iota(jnp.int32, (LANES, TQ), 0)
        for h in range(2):
            pick = (sr == 2 * p + h).astype(F32)
            ckb[h] = jnp.dot(cum[...], pick, precision=HIGHEST, preferred_element_type=F32)

    q = q_ref[0]
    qbs = [_head_mask(q, h).astype(BF16) for h in range(2)]
    kk = lax.broadcasted_iota(jnp.int32, (TQ, TQ), 0)
    qq = lax.broadcasted_iota(jnp.int32, (TQ, TQ), 1)
    causal = kk <= qq

    def scores(h, j):
        st = pl.multiple_of(j * TQ, TQ)
        kblk = kb[pl.ds(st, TQ), :]
        s = lax.dot_general(kblk, qbs[h], _NT, preferred_element_type=F32)
        return s - ckb[h, pl.ds(st, TQ), :]

    def vt_of(h, j):
        return vtb[h * HEAD_DIM:(h + 1) * HEAD_DIM, pl.ds(pl.multiple_of(j * TQ, TQ), TQ)]

    state = []
    for h in range(2):
        s = jnp.where(causal, scores(h, qi), NEG_INF)
        state.extend(_flash_init(s, vt_of(h, qi)))

    def body(j, st):
        out = []
        for h in range(2):
            out.extend(_flash_step(scores(h, j), vt_of(h, j), *st[3 * h:3 * h + 3]))
        return tuple(out)

    st = lax.fori_loop(0, qi, body, tuple(state))
    o = jnp.concatenate([st[2] / st[1], st[5] / st[4]], axis=0)
    o_ref[0] = o.T


def _fox_prompt(q_all, b_kv, small):
    b, t, _ = q_all.shape
    npair = B_HEADS // 2
    qoff = A_HEADS // 2
    nb = t // TQ
    return pl.pallas_call(
        _fox_p_kernel,
        grid=(b, npair, nb),
        in_specs=[pl.BlockSpec((1, TQ, LANES), lambda bi, p, qi: (bi, qi, qoff + p)),
                  pl.BlockSpec((1, t, LANES), lambda bi, p, qi: (bi, 0, p)),
                  pl.BlockSpec((1, t, LANES), lambda bi, p, qi: (bi, 0, npair + p)),
                  pl.BlockSpec((1, t, LANES), lambda bi, p, qi: (bi, 0, 0))],
        out_specs=pl.BlockSpec((1, TQ, LANES), lambda bi, p, qi: (bi, qi, p)),
        out_shape=jax.ShapeDtypeStruct((b, t, B_HEADS * HEAD_DIM), F32),
        scratch_shapes=[pltpu.VMEM((t, LANES), BF16), pltpu.VMEM((LANES, t), BF16),
                        pltpu.VMEM((t, LANES), F32), pltpu.VMEM((2, t, TQ), F32)],
        compiler_params=_cparams(("parallel", "parallel", "arbitrary")),
        name="fox_prompt",
    )(q_all, b_kv, b_kv, small)


def _compress_pair(rows_of, ncp, pe_ref, w1_ref, w2_ref, kind):
    half = CMP_BLOCK // 2
    acc_a = jnp.zeros((ncp, 2 * CMP_HIDDEN), F32)
    acc_b = jnp.zeros((ncp, 2 * CMP_HIDDEN), F32)
    for l in range(half):
        r = rows_of(l)
        xa = (r + pe_ref[kind, l:l + 1, :]).astype(BF16)
        xb = (r + pe_ref[kind, half + l:half + l + 1, :]).astype(BF16)
        acc_a = acc_a + jnp.dot(xa, w1_ref[kind, l], preferred_element_type=F32)
        acc_b = acc_b + jnp.dot(xb, w1_ref[kind, half + l], preferred_element_type=F32)
    pre = acc_a + pltpu.roll(acc_b, ncp - 1, 0)
    hdn = _gelu(pre).astype(BF16)
    return jnp.dot(hdn, w2_ref[kind], preferred_element_type=F32)


def _compress_p_kernel(ck_ref, cv_ref, pe_ref, w1_ref, w2_ref, kc_ref, vc_ref):
    ncp = kc_ref.shape[1]

    def rows_of(ref):
        return lambda l: ref[0, pl.ds(l, ncp, stride=CMP_STRIDE), :]

    kc_ref[0] = _compress_pair(rows_of(ck_ref), ncp, pe_ref, w1_ref, w2_ref, 0).astype(BF16)
    vc_ref[0] = _compress_pair(rows_of(cv_ref), ncp, pe_ref, w1_ref, w2_ref, 1)


def _compress_prompt(c_kv, pe2, w1bd, w2bd):
    b, t, _ = c_kv.shape
    ncp = t // CMP_STRIDE
    const = lambda *shape: pl.BlockSpec(shape, lambda bi: (0,) * len(shape))
    return pl.pallas_call(
        _compress_p_kernel,
        grid=(b,),
        in_specs=[pl.BlockSpec((1, t, LANES), lambda bi: (bi, 0, 0)),
                  pl.BlockSpec((1, t, LANES), lambda bi: (bi, 0, 1)),
                  const(2, CMP_BLOCK, LANES), const(2, CMP_BLOCK, LANES, 2 * CMP_HIDDEN),
                  const(2, 2 * CMP_HIDDEN, LANES)],
        out_specs=[pl.BlockSpec((1, ncp, LANES), lambda bi: (bi, 0, 0)),
                   pl.BlockSpec((1, ncp, LANES), lambda bi: (bi, 0, 0))],
        out_shape=[jax.ShapeDtypeStruct((b, ncp, LANES), BF16), jax.ShapeDtypeStruct((b, ncp, LANES), F32)],
        compiler_params=_cparams(("parallel",)),
        name="compress_prompt",
    )(c_kv, c_kv, pe2, w1bd, w2bd)


def _nsa_p_kernel(far_ref, q0_ref, q1_ref, q2_ref, ckv_ref, cwin_ref, gt_ref, d_ref, bc_ref, cov_ref,
                  kcb, vc_ref, o_ref, vctb, skb, svtb, wkb, wvtb, selsc):
    qi = pl.program_id(1)
    t = skb.shape[0]
    ncp = kcb.shape[1]
    nbs = selsc.shape[1]
    per_tile = TQ // SLC_BLOCK

    @pl.when(qi == 0)
    def _prep():
        vctb[...] = vc_ref[0].T.astype(BF16)
        skb[...] = ckv_ref[0, :, 0:LANES].astype(BF16)
        svtb[...] = ckv_ref[0, :, LANES:2 * LANES].T.astype(BF16)
        wkb[...] = cwin_ref[0, :, 0:LANES].astype(BF16)
        wvtb[...] = cwin_ref[0, :, LANES:2 * LANES].T.astype(BF16)

    q_refs = (q0_ref, q1_ref, q2_ref)
    qpos = qi * TQ + lax.broadcasted_iota(jnp.int32, (1, TQ), 1)
    ci = lax.broadcasted_iota(jnp.int32, (ncp, TQ), 0)
    cok = (qpos >= ci * CMP_STRIDE + (CMP_BLOCK - 1)) & (ci < ncp - 1)

    qbs, ocmp = [], []
    psum = [jnp.zeros((ncp, TQ), F32), jnp.zeros((ncp, TQ), F32)]
    for s in range(C_HEADS):
        kvh = s % 2
        qb = _head_mask(q_refs[s // 2][0], kvh).astype(BF16)
        qbs.append(qb)
        sc = lax.dot_general(kcb[0], qb, _NT, preferred_element_type=F32) + bc_ref[s]
        sc = jnp.where(cok, sc, NEG_INF)
        m = jnp.max(sc, axis=0, keepdims=True)
        e = jnp.where(cok, jnp.exp(sc - m), 0.0)
        den = jnp.sum(e, axis=0, keepdims=True)
        pc = e * jnp.where(den > 0.0, 1.0 / den, 0.0)
        psum[kvh] = psum[kvh] + pc
        ocmp.append(jnp.dot(vctb[kvh * HEAD_DIM:(kvh + 1) * HEAD_DIM, :], pc.astype(BF16),
                            preferred_element_type=F32))

    jj = lax.broadcasted_iota(jnp.int32, (nbs, TQ), 0)
    own = qpos // SLC_BLOCK
    forced = (jj == 0) | (jj == own) | (jj == own - 1)
    for kvh in range(2):
        imp = jnp.dot(cov_ref[...], psum[kvh], precision=HIGHEST, preferred_element_type=F32)
        v = jnp.where(jj <= own, jnp.where(forced, FORCE_SCORE, imp), NEG_INF)
        chosen = (_top_rank(v, jj) < SLC_TOPK) & (jj <= own)
        selsc[kvh] = jnp.where(chosen, 0.0, NEG_INF)

    kk = lax.broadcasted_iota(jnp.int32, (TQ, TQ), 0)
    qq = lax.broadcasted_iota(jnp.int32, (TQ, TQ), 1)
    causal = kk <= qq

    def tile_scores(kref, s, j):
        kblk = kref[pl.ds(pl.multiple_of(j * TQ, TQ), TQ), :]
        return lax.dot_general(kblk, qbs[s], _NT, preferred_element_type=F32)

    def vt_of(vref, kvh, j):
        return vref[kvh * HEAD_DIM:(kvh + 1) * HEAD_DIM, pl.ds(pl.multiple_of(j * TQ, TQ), TQ)]

    def sel_rows(kvh, j):
        rows = [jnp.broadcast_to(selsc[kvh, pl.ds(j * per_tile + r, 1), :], (SLC_BLOCK, TQ))
                for r in range(per_tile)]
        return jnp.concatenate(rows, axis=0)

    for pair in range(C_HEADS // 2):
        slots = (2 * pair, 2 * pair + 1)
        state = []
        for s in slots:
            kvh = s % 2
            sc = tile_scores(skb, s, qi) + d_ref[s, 0] + sel_rows(kvh, qi)
            state.extend(_flash_init(jnp.where(causal, sc, NEG_INF), vt_of(svtb, kvh, qi)))

        def body(j, st, slots=slots):
            out = []
            for n, s in enumerate(slots):
                kvh = s % 2
                bias = jnp.where(j == qi - 1, d_ref[s, 1], far_ref[s])
                sc = tile_scores(skb, s, j) + bias + sel_rows(kvh, j)
                out.extend(_flash_step(sc, vt_of(svtb, kvh, j), *st[3 * n:3 * n + 3]))
            return tuple(out)

        st_slc = lax.fori_loop(0, qi, body, tuple(state))

        outs = []
        for n, s in enumerate(slots):
            kvh = s % 2
            sc = jnp.where(causal, tile_scores(wkb, s, qi) + d_ref[s, 0], NEG_INF)
            m, l, acc = _flash_init(sc, vt_of(wvtb, kvh, qi))
            j1 = jnp.maximum(qi - 1, 0)
            sc = jnp.where(qi >= 1, tile_scores(wkb, s, j1) + d_ref[s, 1], NEG_INF)
            m, l, acc = _flash_step(sc, vt_of(wvtb, kvh, j1), m, l, acc)
            j2 = jnp.maximum(qi - 2, 0)
            sc = jnp.where((kk > qq) & (qi >= 2), tile_scores(wkb, s, j2) + far_ref[s], NEG_INF)
            m, l, acc = _flash_step(sc, vt_of(wvtb, kvh, j2), m, l, acc)
            o_win = acc / l
            o_slc = st_slc[3 * n + 2] / st_slc[3 * n + 1]
            grow = B_HEADS + C_SLOT_HEADS[s] * N_GATES
            g = [jax.nn.sigmoid(gt_ref[0, grow + n_:grow + n_ + 1, :]) for n_ in range(N_GATES)]
            outs.append(g[0] * ocmp[s] + g[1] * o_slc + g[2] * o_win)
        o_ref[0, :, pair * LANES:(pair + 1) * LANES] = jnp.concatenate(outs, axis=0).T


def _nsa_prompt(q_all, c_kv, c_win, small_t, dtiles, far, bc, cover_t, kc, vct):
    b, t, _ = q_all.shape
    nq = t // TQ
    ncp = t // CMP_STRIDE
    nbs = t // SLC_BLOCK
    qoff = (A_HEADS + B_HEADS) // 2
    const = lambda *shape: pl.BlockSpec(shape, lambda bi, qi: (0,) * len(shape))
    qspec = lambda m: pl.BlockSpec((1, TQ, LANES), lambda bi, qi: (bi, qi, qoff + m))
    return pl.pallas_call(
        _nsa_p_kernel,
        grid=(b, nq),
        in_specs=[pl.BlockSpec(memory_space=pltpu.SMEM), qspec(0), qspec(1), qspec(2),
                  pl.BlockSpec((1, t, 2 * LANES), lambda bi, qi: (bi, 0, 1)),
                  pl.BlockSpec((1, t, CWIN_COLS), lambda bi, qi: (bi, 0, 0)),
                  pl.BlockSpec((1, SMALL_COLS, TQ), lambda bi, qi: (bi, 0, qi)),
                  const(C_HEADS, 2, TQ, TQ),
                  pl.BlockSpec((C_HEADS, ncp, TQ), lambda bi, qi: (0, 0, qi)),
                  const(nbs, ncp),
                  pl.BlockSpec((1, ncp, LANES), lambda bi, qi: (bi, 0, 0)),
                  pl.BlockSpec((1, ncp, LANES), lambda bi, qi: (bi, 0, 0))],
        out_specs=pl.BlockSpec((1, TQ, C_HEADS * HEAD_DIM), lambda bi, qi: (bi, qi, 0)),
        out_shape=jax.ShapeDtypeStruct((b, t, C_HEADS * HEAD_DIM), F32),
        scratch_shapes=[pltpu.VMEM((LANES, ncp), BF16),
                        pltpu.VMEM((t, LANES), BF16), pltpu.VMEM((LANES, t), BF16),
                        pltpu.VMEM((t, LANES), BF16), pltpu.VMEM((LANES, t), BF16),
                        pltpu.VMEM((2, nbs, TQ), F32)],
        compiler_params=_cparams(("parallel", "arbitrary")),
        name="nsa_prompt",
    )(far, q_all, q_all, q_all, c_kv, c_win, small_t, dtiles, bc, cover_t, kc, vct)


FF_CHUNK = 256


def _rms(x, g):
    ms = jnp.mean(x * x, axis=-1, keepdims=True)
    return (x * lax.rsqrt(ms + RMS_EPS)) * g


def _ffn_kernel(sample, blocks_per_seq, final, x_ref, ya_ref, yb_ref, yc_ref, woa_ref, wob_ref, woc_ref,
                g2_ref, gf_ref, wg_ref, wu_ref, wd_ref, cw_ref, cb_ref, *rest):
    if sample:
        p1_ref, p2_ref, o_ref, gk_ref, x1_s, h2_s, acc_s = rest
    else:
        o_ref, gk_ref, x1_s, h2_s, acc_s, halo_s = rest
    i = pl.program_id(0)
    c = pl.program_id(1)
    tm = x_ref.shape[0]

    @pl.when(c == 0)
    def _attn_out():
        y = (jnp.dot(ya_ref[...].astype(BF16), woa_ref[...], preferred_element_type=F32)
             + jnp.dot(yb_ref[...].astype(BF16), wob_ref[...], preferred_element_type=F32)
             + jnp.dot(yc_ref[...].astype(BF16), woc_ref[...], preferred_element_type=F32))
        x1 = x_ref[...] + y
        x1_s[...] = x1
        h2_s[...] = _rms(x1, g2_ref[...]).astype(BF16)
        acc_s[...] = jnp.zeros(acc_s.shape, F32)

    hb = h2_s[...]
    g = jnp.dot(hb, wg_ref[...], preferred_element_type=F32)
    u = jnp.dot(hb, wu_ref[...], preferred_element_type=F32)
    row = lax.broadcasted_iota(jnp.int32, g.shape, 0)
    r1 = pltpu.roll(g, 1, 0)
    r2 = pltpu.roll(g, 2, 0)
    if sample:
        t = row % SUBLANES
        gs1 = jnp.where(t >= 1, r1, p1_ref[...])
        gs2 = jnp.where(t >= 2, r2, p2_ref[...])
        gk_ref[...] = g
    else:
        halo = jnp.where(i % blocks_per_seq == 0, 0.0, halo_s[c])
        gs1 = jnp.where(row >= 1, r1, halo[SUBLANES - 1:SUBLANES, :])
        gs2 = jnp.where(row >= 2, r2, jnp.where(row == 0, halo[SUBLANES - 2:SUBLANES - 1, :],
                                                  halo[SUBLANES - 1:SUBLANES, :]))
        tail = g[tm - SUBLANES:tm, :]
        halo_s[c] = tail
        gk_ref[0] = tail
    conv = cb_ref[...] + cw_ref[0:1, :] * gs2
    conv = conv + cw_ref[1:2, :] * gs1
    conv = conv + cw_ref[2:3, :] * g
    act = (_gelu(conv) * u).astype(BF16)
    acc_s[...] += jnp.dot(act, wd_ref[...], preferred_element_type=F32)

    @pl.when(c == pl.num_programs(1) - 1)
    def _finish():
        x2 = x1_s[...] + acc_s[...]
        o_ref[...] = _rms(x2, gf_ref[...]) if final else x2


def _ffn(x2d, ya, yb, yc, woa, wob, woc, g2, gf, wg, wu, wd, cw, cb, *, sample, blocks_per_seq=1, final=False,
         inj=None, tm=512):
    n = x2d.shape[0]
    tm = min(tm, n)
    nc = D_FF // FF_CHUNK
    rows = lambda w: pl.BlockSpec((tm, w), lambda i, c: (i, 0))
    const = lambda *shape: pl.BlockSpec(shape, lambda i, c: (0,) * len(shape))
    in_specs = [rows(D_MODEL), rows(ya.shape[1]), rows(yb.shape[1]), rows(yc.shape[1]),
                const(*woa.shape), const(*wob.shape), const(*woc.shape), const(1, D_MODEL), const(1, D_MODEL),
                pl.BlockSpec((D_MODEL, FF_CHUNK), lambda i, c: (0, c)),
                pl.BlockSpec((D_MODEL, FF_CHUNK), lambda i, c: (0, c)),
                pl.BlockSpec((FF_CHUNK, D_MODEL), lambda i, c: (c, 0)),
                pl.BlockSpec((CONV_W, FF_CHUNK), lambda i, c: (0, c)),
                pl.BlockSpec((1, FF_CHUNK), lambda i, c: (0, c))]
    args = [x2d, ya, yb, yc, woa, wob, woc, g2.reshape(1, D_MODEL), gf.reshape(1, D_MODEL), wg, wu, wd, cw,
            cb.reshape(1, D_FF)]
    scratch = [pltpu.VMEM((tm, D_MODEL), F32), pltpu.VMEM((tm, D_MODEL), BF16), pltpu.VMEM((tm, D_MODEL), F32)]
    if sample:
        in_specs += [pl.BlockSpec((tm, FF_CHUNK), lambda i, c: (i, c))] * 2
        args += list(inj)
        gk_spec = pl.BlockSpec((tm, FF_CHUNK), lambda i, c: (i, c))
        gk_shape = jax.ShapeDtypeStruct((n, D_FF), F32)
    else:
        scratch.append(pltpu.VMEM((nc, SUBLANES, FF_CHUNK), F32))
        gk_spec = pl.BlockSpec((1, SUBLANES, FF_CHUNK), lambda i, c: (i, 0, c))
        gk_shape = jax.ShapeDtypeStruct((n // tm, SUBLANES, D_FF), F32)
    return pl.pallas_call(
        functools.partial(_ffn_kernel, sample, blocks_per_seq, final),
        grid=(n // tm, nc),
        in_specs=in_specs,
        out_specs=[pl.BlockSpec((tm, D_MODEL), lambda i, c: (i, 0)), gk_spec],
        out_shape=[jax.ShapeDtypeStruct((n, D_MODEL), F32), gk_shape],
        scratch_shapes=scratch,
        compiler_params=_cparams(("arbitrary", "arbitrary")),
        name="ffn_sample" if sample else "ffn_prompt",
    )(*args)


N_PAGES = 16
PAST_LEN = N_PAGES * PAGE_SIZE
DEC_SEQ = SUBLANES


def _top_rank_lanes(vals, n):
    lane = lax.broadcasted_iota(jnp.int32, vals.shape, 1)
    rank = jnp.zeros(vals.shape, F32)
    for j in range(n):
        vj = vals[:, j:j + 1]
        beats = (vj > vals) | ((vj == vals) & (j < lane))
        rank = rank + beats.astype(F32)
    return rank


def _stack_heads(q, lane_heads):
    return jnp.concatenate([_head_mask(q, h) for h in lane_heads], axis=0)


def _pad_rows(x, rows=LANES):
    return jnp.concatenate([x, jnp.zeros((rows - x.shape[0], x.shape[1]), x.dtype)], axis=0)


def _softmax_pv(parts):
    m = parts[0][0].max(axis=1, keepdims=True)
    for s, _ in parts[1:]:
        m = jnp.maximum(m, s.max(axis=1, keepdims=True))
    l = 0.0
    o = 0.0
    for s, v in parts:
        p = jnp.exp(s - m)
        l = l + jnp.sum(p, axis=1, keepdims=True)
        o = o + jnp.dot(p.astype(BF16), v, preferred_element_type=F32)
    return o / l


def _pair_out(o, r0, r1):
    lane = lax.broadcasted_iota(jnp.int32, (DEC_SEQ, LANES), 1)
    return jnp.where(lane < HEAD_DIM, o[r0:r0 + DEC_SEQ], o[r1:r1 + DEC_SEQ])


def _moba_s_kernel(pt_ref, q_ref, new_ref, bp_ref, bn_ref, *rest):
    pages = rest[:N_PAGES]
    o_ref = rest[N_PAGES]
    nblk = PAST_LEN // MOBA_BLOCK
    ppb = MOBA_BLOCK // PAGE_SIZE
    width = A_HEADS * HEAD_DIM
    q = q_ref[0]
    lane = lax.broadcasted_iota(jnp.int32, (2 * DEC_SEQ, LANES), 1)
    t_row = lax.broadcasted_iota(jnp.int32, (2 * DEC_SEQ, LANES), 0) % DEC_SEQ
    means = []
    for j in range(nblk):
        tot = 0.0
        for pg in pages[j * ppb:(j + 1) * ppb]:
            tot = tot + jnp.sum(pg[0, :, 0:width], axis=0, keepdims=True)
        means.append(tot / MOBA_BLOCK)
    kmean = jnp.concatenate(means, axis=0)
    outs = []
    for p in range(A_HEADS // 2):
        cols = slice(p * LANES, (p + 1) * LANES)
        q2 = _stack_heads(q[:, cols], (0, 1))
        gate = lax.dot_general(q2, _pad_rows(kmean[:, cols]), _NT, precision=HIGHEST,
                               preferred_element_type=F32)
        gm = jnp.where(lane < nblk, gate, NEG_INF)
        chosen = (_top_rank_lanes(gm, nblk) < MOBA_TOPK) & (lane < nblk)
        neg = jnp.where(chosen, 0.0, NEG_INF)
        q2b = q2.astype(BF16)
        kb = jnp.concatenate([pg[0, :, cols] for pg in pages], axis=0).astype(BF16)
        vb = jnp.concatenate([pg[0, :, width + p * LANES:width + (p + 1) * LANES] for pg in pages],
                             axis=0).astype(BF16)
        s_past = lax.dot_general(q2b, kb, _NT, preferred_element_type=F32) + bp_ref[p]
        s_past = s_past + jnp.concatenate([jnp.broadcast_to(neg[:, j:j + 1], (2 * DEC_SEQ, MOBA_BLOCK))
                                           for j in range(nblk)], axis=1)
        knp = _pad_rows(new_ref[0, :, cols]).astype(BF16)
        vnp = _pad_rows(new_ref[0, :, width + p * LANES:width + (p + 1) * LANES]).astype(BF16)
        s_new = lax.dot_general(q2b, knp, _NT, preferred_element_type=F32) + bn_ref[p]
        s_new = jnp.where(lane <= t_row, s_new, NEG_INF)
        o = _softmax_pv([(s_past, vb), (s_new, vnp)])
        outs.append(_pair_out(o, 0, DEC_SEQ))
    o_ref[0] = jnp.concatenate(outs, axis=1)


def _page_specs(width, nb_args):
    def spec(i):
        return pl.BlockSpec((1, PAGE_SIZE, width), lambda b, pt: (pt[b * N_PAGES + i], 0, 0))
    return [spec(i) for i in range(N_PAGES)]


def _moba_sample(pt, q_s, a_new, cache, bias_past, bias_new):
    nbt = q_s.shape[0]
    row = lambda w: pl.BlockSpec((1, DEC_SEQ, w), lambda b, pt: (b, 0, 0))
    const = lambda *shape: pl.BlockSpec(shape, lambda b, pt: (0,) * len(shape))
    gs = pltpu.PrefetchScalarGridSpec(
        num_scalar_prefetch=1, grid=(nbt,),
        in_specs=[row(Q_COLS), row(AKV_COLS), const(*bias_past.shape), const(*bias_new.shape)]
        + _page_specs(AKV_COLS, 1),
        out_specs=row(A_HEADS * HEAD_DIM))
    return pl.pallas_call(
        _moba_s_kernel, grid_spec=gs,
        out_shape=jax.ShapeDtypeStruct((nbt, DEC_SEQ, A_HEADS * HEAD_DIM), F32),
        compiler_params=_cparams(("arbitrary",)), name="moba_sample",
    )(pt, q_s, a_new, bias_past, bias_new, *([cache] * N_PAGES))


def _fox_s_kernel(pt_ref, q_ref, new_ref, small_ref, sfx_ref, *rest):
    pages = rest[:N_PAGES]
    lpages = rest[N_PAGES:2 * N_PAGES]
    o_ref = rest[2 * N_PAGES]
    width = B_HEADS * HEAD_DIM
    qoff = A_HEADS * HEAD_DIM
    r16 = 2 * DEC_SEQ
    lane = lax.broadcasted_iota(jnp.int32, (r16, LANES), 1)
    t_row = lax.broadcasted_iota(jnp.int32, (r16, LANES), 0) % DEC_SEQ

    x = jnp.concatenate([lp[0] for lp in lpages], axis=0)
    hi = x.astype(BF16)
    r1 = x - hi.astype(F32)
    mid = r1.astype(BF16)
    lo = (r1 - mid.astype(F32)).astype(BF16)
    parts = jnp.dot(jnp.concatenate([hi, mid, lo], axis=0), sfx_ref[...], preferred_element_type=F32)
    sfx = parts[0:N_PAGES] + parts[N_PAGES:2 * N_PAGES] + parts[2 * N_PAGES:3 * N_PAGES]
    tot = sfx[:, B_HEADS * PAGE_SIZE:]
    offs = [None] * N_PAGES
    run = jnp.zeros((1, LANES), F32)
    for pg in range(N_PAGES - 1, -1, -1):
        offs[pg] = run
        run = run + tot[pg:pg + 1, :]

    def decay_row(h):
        return jnp.concatenate([sfx[pg:pg + 1, h * PAGE_SIZE:(h + 1) * PAGE_SIZE] + offs[pg][:, h:h + 1]
                                for pg in range(N_PAGES)], axis=1)

    lfn = small_ref[0]
    sub = lax.broadcasted_iota(jnp.int32, (DEC_SEQ, LANES), 0)
    lane8 = lax.broadcasted_iota(jnp.int32, (DEC_SEQ, LANES), 1)
    cs = lfn
    for sh in (1, 2, 4):
        cs = cs + jnp.where(sub >= sh, pltpu.roll(cs, sh, 0), 0.0)

    q = q_ref[0]
    outs = []
    for p in range(B_HEADS // 2):
        cols = slice(p * LANES, (p + 1) * LANES)
        q2b = _stack_heads(q[:, qoff + p * LANES:qoff + (p + 1) * LANES], (0, 1)).astype(BF16)
        heads = (2 * p, 2 * p + 1)
        a_rows = jnp.concatenate([jnp.broadcast_to(decay_row(h), (DEC_SEQ, PAST_LEN)) for h in heads], axis=0)
        d_col = jnp.concatenate([cs[:, h:h + 1] for h in heads], axis=0)
        d_row = jnp.concatenate(
            [jnp.broadcast_to(jnp.sum(jnp.where(sub <= lane8, jnp.broadcast_to(lfn[:, h:h + 1], (DEC_SEQ, LANES)),
                                                0.0), axis=0, keepdims=True), (DEC_SEQ, LANES))
             for h in heads], axis=0)
        kb = jnp.concatenate([pg[0, :, cols] for pg in pages], axis=0).astype(BF16)
        vb = jnp.concatenate([pg[0, :, width + p * LANES:width + (p + 1) * LANES] for pg in pages],
                             axis=0).astype(BF16)
        s_past = lax.dot_general(q2b, kb, _NT, preferred_element_type=F32) + (a_rows + d_col)
        knp = _pad_rows(new_ref[0, :, cols]).astype(BF16)
        vnp = _pad_rows(new_ref[0, :, width + p * LANES:width + (p + 1) * LANES]).astype(BF16)
        s_new = lax.dot_general(q2b, knp, _NT, preferred_element_type=F32) + (d_col - d_row)
        s_new = jnp.where(lane <= t_row, s_new, NEG_INF)
        o = _softmax_pv([(s_past, vb), (s_new, vnp)])
        outs.append(_pair_out(o, 0, DEC_SEQ))
    o_ref[0] = jnp.concatenate(outs, axis=1)


def _suffix_matrix():
    rows = np.arange(PAGE_SIZE * B_HEADS)
    tok_r, head_r = rows // B_HEADS, rows % B_HEADS
    cols = np.arange(PAGE_SIZE * B_HEADS)
    head_c, tok_c = cols // PAGE_SIZE, cols % PAGE_SIZE
    m = (head_r[:, None] == head_c[None, :]) & (tok_r[:, None] > tok_c[None, :])
    tot = head_r[:, None] == np.arange(LANES)[None, :]
    return jnp.asarray(np.concatenate([m, tot], axis=1), BF16)


def _fox_sample(pt, q_s, b_new, small_s, cache, cache_logf):
    nbt = q_s.shape[0]
    lw = PAGE_SIZE * B_HEADS
    row = lambda w: pl.BlockSpec((1, DEC_SEQ, w), lambda b, pt: (b, 0, 0))
    sfx = _suffix_matrix()
    lspecs = [pl.BlockSpec((1, 1, lw), functools.partial(lambda i, b, pt: (pt[b * N_PAGES + i], 0, 0), i))
              for i in range(N_PAGES)]
    gs = pltpu.PrefetchScalarGridSpec(
        num_scalar_prefetch=1, grid=(nbt,),
        in_specs=[row(Q_COLS), row(BKV_COLS), row(SMALL_COLS),
                  pl.BlockSpec(sfx.shape, lambda b, pt: (0, 0))] + _page_specs(BKV_COLS, 1) + lspecs,
        out_specs=row(B_HEADS * HEAD_DIM))
    return pl.pallas_call(
        _fox_s_kernel, grid_spec=gs,
        out_shape=jax.ShapeDtypeStruct((nbt, DEC_SEQ, B_HEADS * HEAD_DIM), F32),
        compiler_params=_cparams(("arbitrary",)), name="fox_sample",
    )(pt, q_s, b_new, small_s, sfx, *([cache] * N_PAGES), *([cache_logf] * N_PAGES))


def _nsa_s_kernel(pt_ref, q_ref, new_ref, wnew_ref, gate_ref, win_ref, bc_ref, bp_ref, bn_ref, bw_ref,
                  cov_ref, exp_ref, pe_ref, w1_ref, w2_ref, *rest):
    pages = rest[:N_PAGES]
    o_ref = rest[N_PAGES]
    ctok = rest[N_PAGES + 1]
    ncp = PAST_LEN // CMP_STRIDE
    nbs_past = PAST_LEN // SLC_BLOCK
    qoff = (A_HEADS + B_HEADS) * HEAD_DIM
    nrow = C_HEADS * DEC_SEQ
    lane = lax.broadcasted_iota(jnp.int32, (nrow, LANES), 1)
    t_row = lax.broadcasted_iota(jnp.int32, (nrow, LANES), 0) % DEC_SEQ

    for i, pg in enumerate(pages):
        for kind in range(2):
            ctok[kind, i * PAGE_SIZE:(i + 1) * PAGE_SIZE, :] = pg[0, :, kind * LANES:(kind + 1) * LANES]

    def rows_of(kind):
        return lambda l: ctok[kind, pl.ds(l, ncp, stride=CMP_STRIDE), :]

    kc = _compress_pair(rows_of(0), ncp, pe_ref, w1_ref, w2_ref, 0).astype(BF16)
    vc = _compress_pair(rows_of(1), ncp, pe_ref, w1_ref, w2_ref, 1).astype(BF16)

    q = q_ref[0]
    q2 = jnp.concatenate([_head_mask(q[:, qoff + (s // 2) * LANES:qoff + (s // 2 + 1) * LANES], s % 2)
                          for s in range(C_HEADS)], axis=0)
    q2b = q2.astype(BF16)

    sc = lax.dot_general(q2b, kc, _NT, preferred_element_type=F32) + bc_ref[...]
    sc = jnp.where(lane < ncp - 1, sc, NEG_INF)
    m = sc.max(axis=1, keepdims=True)
    e = jnp.exp(sc - m)
    pc = e / jnp.sum(e, axis=1, keepdims=True)
    o_cmp = jnp.dot(pc.astype(BF16), vc, preferred_element_type=F32)

    psum = [sum(pc[s * DEC_SEQ:(s + 1) * DEC_SEQ] for s in range(kvh, C_HEADS, 2)) for kvh in range(2)]
    imp = jnp.dot(jnp.concatenate(psum, axis=0), cov_ref[...], precision=HIGHEST, preferred_element_type=F32)
    lane16 = lax.broadcasted_iota(jnp.int32, (2 * DEC_SEQ, LANES), 1)
    forced = (lane16 == 0) | (lane16 == nbs_past) | (lane16 == nbs_past - 1)
    v = jnp.where(lane16 <= nbs_past, jnp.where(forced, FORCE_SCORE, imp), NEG_INF)
    chosen = (_top_rank_lanes(v, nbs_past + 1) < SLC_TOPK) & (lane16 <= nbs_past)
    chosen_f = jnp.where(chosen, 1.0, 0.0)
    mask16 = jnp.dot(chosen_f.astype(BF16), exp_ref[...], preferred_element_type=F32)
    mask_past = jnp.concatenate([mask16[(s % 2) * DEC_SEQ:(s % 2 + 1) * DEC_SEQ] for s in range(C_HEADS)], axis=0)
    mask_new = jnp.concatenate(
        [jnp.broadcast_to(chosen_f[(s % 2) * DEC_SEQ:(s % 2 + 1) * DEC_SEQ, nbs_past:nbs_past + 1],
                          (DEC_SEQ, LANES)) for s in range(C_HEADS)], axis=0)

    kb = jnp.concatenate([pg[0, :, 2 * LANES:3 * LANES] for pg in pages], axis=0).astype(BF16)
    vb = jnp.concatenate([pg[0, :, 3 * LANES:4 * LANES] for pg in pages], axis=0).astype(BF16)
    s_past = lax.dot_general(q2b, kb, _NT, preferred_element_type=F32) + bp_ref[...]
    s_past = jnp.where(mask_past > 0.5, s_past, NEG_INF)
    knp = _pad_rows(new_ref[0, :, 2 * LANES:3 * LANES]).astype(BF16)
    vnp = _pad_rows(new_ref[0, :, 3 * LANES:4 * LANES]).astype(BF16)
    s_new = lax.dot_general(q2b, knp, _NT, preferred_element_type=F32) + bn_ref[...]
    s_new = jnp.where((lane <= t_row) & (mask_new > 0.5), s_new, NEG_INF)
    o_slc = _softmax_pv([(s_past, vb), (s_new, vnp)])

    wk = win_ref[0, :, 0:LANES].astype(BF16)
    wv = win_ref[0, :, LANES:2 * LANES].astype(BF16)
    wlane = lax.broadcasted_iota(jnp.int32, (nrow, WINDOW), 1)
    wt = lax.broadcasted_iota(jnp.int32, (nrow, WINDOW), 0) % DEC_SEQ
    s_w = lax.dot_general(q2b, wk, _NT, preferred_element_type=F32) + bw_ref[...]
    s_w = jnp.where(wlane > wt, s_w, NEG_INF)
    wkn = _pad_rows(wnew_ref[0, :, 0:LANES]).astype(BF16)
    wvn = _pad_rows(wnew_ref[0, :, LANES:2 * LANES]).astype(BF16)
    s_wn = lax.dot_general(q2b, wkn, _NT, preferred_element_type=F32) + bn_ref[...]
    s_wn = jnp.where(lane <= t_row, s_wn, NEG_INF)
    o_win = _softmax_pv([(s_w, wv), (s_wn, wvn)])

    g = jax.nn.sigmoid(gate_ref[0])
    o = g[:, 0:1] * o_cmp + g[:, 1:2] * o_slc + g[:, 2:3] * o_win
    o_ref[0] = jnp.concatenate([_pair_out(o, 2 * m_ * DEC_SEQ, (2 * m_ + 1) * DEC_SEQ)
                                for m_ in range(C_HEADS // 2)], axis=1)


def _nsa_sample(pt, q_s, c_new, w_new, gates, win_state, layer, cache, tables, pe2, w1bd, w2bd):
    nbt = q_s.shape[0]
    bc, bp, bn, bw, cov, expand = tables
    row = lambda r, w: pl.BlockSpec((1, r, w), lambda b, pt: (b, 0, 0))
    win_spec = pl.BlockSpec((1, WINDOW, CWIN_COLS), lambda b, pt: (layer * nbt + b, 0, 0))
    const = lambda a: pl.BlockSpec(a.shape, lambda b, pt: (0,) * a.ndim)
    gs = pltpu.PrefetchScalarGridSpec(
        num_scalar_prefetch=1, grid=(nbt,),
        in_specs=[row(DEC_SEQ, Q_COLS), row(DEC_SEQ, CKV_COLS), row(DEC_SEQ, CWIN_COLS),
                  row(C_HEADS * DEC_SEQ, LANES), win_spec,
                  const(bc), const(bp), const(bn), const(bw), const(cov), const(expand),
                  const(pe2), const(w1bd), const(w2bd)] + _page_specs(CKV_COLS, 1),
        out_specs=row(DEC_SEQ, C_HEADS * HEAD_DIM),
        scratch_shapes=[pltpu.VMEM((2, PAST_LEN, LANES), F32)])
    return pl.pallas_call(
        _nsa_s_kernel, grid_spec=gs,
        out_shape=jax.ShapeDtypeStruct((nbt, DEC_SEQ, C_HEADS * HEAD_DIM), F32),
        compiler_params=_cparams(("arbitrary",)), name="nsa_sample",
    )(pt, q_s, c_new, w_new, gates, win_state, bc, bp, bn, bw, cov, expand, pe2, w1bd, w2bd,
      *([cache] * N_PAGES))


def _sample_bias(bd, key_pos):
    h = bd.shape[0]
    hh = np.repeat(np.arange(h), DEC_SEQ)[:, None]
    qpos = PAST_LEN + np.tile(np.arange(DEC_SEQ), h)[:, None]
    return bd[hh, np.clip(qpos - np.asarray(key_pos)[None, :], 0, MAX_DISTANCE)]


def _sample_tables_moba(bd_a):
    past = _sample_bias(bd_a, np.arange(PAST_LEN)).reshape(A_HEADS // 2, 2 * DEC_SEQ, PAST_LEN)
    new = _sample_bias(bd_a, PAST_LEN + np.arange(LANES)).reshape(A_HEADS // 2, 2 * DEC_SEQ, LANES)
    return past, new


def _sample_tables_nsa(bd_c):
    ncp = PAST_LEN // CMP_STRIDE
    nbs = PAST_LEN // SLC_BLOCK
    bc = _sample_bias(bd_c, np.arange(ncp) * CMP_STRIDE + CMP_BLOCK - 1)
    bp = _sample_bias(bd_c, np.arange(PAST_LEN))
    bn = _sample_bias(bd_c, PAST_LEN + np.arange(LANES))
    bw = _sample_bias(bd_c, PAST_LEN - WINDOW + np.arange(WINDOW))
    ci = np.arange(ncp)[:, None]
    bj = np.arange(LANES)[None, :]
    cov = ((ci * CMP_STRIDE < bj * SLC_BLOCK + SLC_BLOCK) & (ci * CMP_STRIDE + CMP_BLOCK > bj * SLC_BLOCK)
           & (ci < ncp - 1) & (bj < nbs))
    expand = np.arange(LANES)[:, None] == (np.arange(PAST_LEN)[None, :] // SLC_BLOCK)
    return bc, bp, bn, bw, jnp.asarray(cov, F32), jnp.asarray(expand, BF16)


def _prep_compress(cmp_pe, cmp_w1, cmp_w2):
    pe2 = jnp.concatenate([cmp_pe, cmp_pe], axis=-1)
    z1 = jnp.zeros_like(cmp_w1)
    w1bd = jnp.concatenate([jnp.concatenate([cmp_w1, z1], axis=-1),
                            jnp.concatenate([z1, cmp_w1], axis=-1)], axis=-2)
    z2 = jnp.zeros_like(cmp_w2)
    w2bd = jnp.concatenate([jnp.concatenate([cmp_w2, z2], axis=-1),
                            jnp.concatenate([z2, cmp_w2], axis=-1)], axis=-2)
    return pe2, w1bd.astype(BF16), w2bd.astype(BF16)


def _cover_t(nbs, ncp):
    cstart = np.arange(ncp)[None, :] * CMP_STRIDE
    bstart = np.arange(nbs)[:, None] * SLC_BLOCK
    cov = (cstart < bstart + SLC_BLOCK) & (cstart + CMP_BLOCK > bstart) & (np.arange(ncp)[None, :] < ncp - 1)
    return jnp.asarray(cov, F32)


def _cmp_bias_prompt(bd, t):
    ncp = t // CMP_STRIDE
    cend = np.arange(ncp)[:, None] * CMP_STRIDE + CMP_BLOCK - 1
    idx = np.clip(np.arange(t)[None, :] - cend, 0, MAX_DISTANCE)
    return bd[:, idx]


_SPLIT_SIZES = (256, 256, 256, 384, 384, 384, 6, 384, 128, 128, 128, 128, 128, 128, 18)


def _prep_w_in(w_in):
    offs = np.cumsum(_SPLIT_SIZES)[:-1].tolist()
    aq, ak, av, bq, bk, bv, bf, cq, cck, ccv, csk, csv, cwk, cwv, cg = jnp.split(w_in, offs, axis=-1)
    lead = cq.shape[:-1]
    cq = cq.reshape(lead + (C_HEADS, HEAD_DIM))[..., C_SLOT_HEADS, :].reshape(lead + (C_HEADS * HEAD_DIM,))
    scale = HEAD_DIM ** -0.5
    pad = jnp.zeros(lead + (SMALL_COLS - B_HEADS - C_HEADS * N_GATES,), w_in.dtype)
    w = jnp.concatenate([aq * scale, bq * scale, cq * scale, ak, av, bk, bv,
                         cck, ccv, csk, csv, cwk, cwv, bf, cg, pad], axis=-1)
    return w.astype(BF16)


def _prep_small_bias(b_f, b_gate):
    pad = jnp.zeros(b_f.shape[:-1] + (SMALL_COLS - B_HEADS - C_HEADS * N_GATES,), F32)
    return jnp.concatenate([b_f, b_gate, pad], axis=-1)[:, None, :]


def _bias_by_distance(rel_bias):
    return rel_bias[_BUCKETS].T


def _toeplitz_tiles(bd):
    k = np.arange(TQ)[:, None]
    q = np.arange(TQ)[None, :]
    i0 = np.clip(q - k, 0, MAX_DISTANCE)
    i1 = np.minimum(TQ + q - k, MAX_DISTANCE)
    return jnp.stack([bd[:, i0], bd[:, i1]], axis=1)


def _prep_w_out(w_out):
    wa = w_out[:, :A_HEADS * HEAD_DIM]
    wb = w_out[:, A_HEADS * HEAD_DIM:(A_HEADS + B_HEADS) * HEAD_DIM]
    wc = w_out[:, (A_HEADS + B_HEADS) * HEAD_DIM:]
    d = wc.shape[0]
    wc = wc.reshape(d, C_HEADS, HEAD_DIM, D_MODEL)[:, C_SLOT_HEADS].reshape(d, C_HEADS * HEAD_DIM, D_MODEL)
    return wa.astype(BF16), wb.astype(BF16), wc.astype(BF16)


def kernel(x_prompt, x_sample, cache_a_kv, cache_b_kv, cache_b_logf, cache_c_kv, state_c_win, state_ffn_conv,
           page_table, rel_bias, ln1_g, w_in, b_f, b_gate, cmp_pe, cmp_w1, cmp_w2, w_out, ln2_g, w_gate, w_up,
           conv_w, conv_b, w_down, final_g):
    depth = w_in.shape[0]
    b, t, _ = x_prompt.shape
    nbt, ts, _ = x_sample.shape
    n_pool = cache_a_kv.shape[1]
    assert ts == DEC_SEQ and page_table.shape[1] == N_PAGES and cache_a_kv.shape[2] == PAGE_SIZE
    assert t % TQ == 0 and state_c_win.shape[2] == WINDOW and t >= WINDOW
    ffn_tm = 512

    w_in_p = _prep_w_in(w_in)
    small_bias = _prep_small_bias(b_f, b_gate)
    pe2, w1bd, w2bd = _prep_compress(cmp_pe, cmp_w1, cmp_w2)
    woa, wob, woc = _prep_w_out(w_out)
    wg, wu, wd = w_gate.astype(BF16), w_up.astype(BF16), w_down.astype(BF16)

    bd = _bias_by_distance(rel_bias)
    bd_a = bd[:A_HEADS]
    bd_c = bd[A_HEADS:][np.asarray(C_SLOT_HEADS)]
    tiles_a, far_a = _toeplitz_tiles(bd_a), bd_a[:, MAX_DISTANCE]
    tiles_c, far_c = _toeplitz_tiles(bd_c), bd_c[:, MAX_DISTANCE]
    bc_p = _cmp_bias_prompt(bd_c, t)
    cover_t = _cover_t(t // SLC_BLOCK, t // CMP_STRIDE)
    moba_tabs = _sample_tables_moba(bd_a)
    nsa_tabs = _sample_tables_nsa(bd_c)

    cache_a = cache_a_kv.reshape(depth * n_pool, PAGE_SIZE, AKV_COLS)
    cache_b = cache_b_kv.reshape(depth * n_pool, PAGE_SIZE, BKV_COLS)
    cache_lf = cache_b_logf.reshape(depth * n_pool, 1, PAGE_SIZE * B_HEADS)
    cache_c = cache_c_kv.reshape(depth * n_pool, PAGE_SIZE, CKV_COLS)
    win_all = state_c_win.reshape(depth * nbt, WINDOW, CWIN_COLS)
    pt_flat = page_table.reshape(-1).astype(jnp.int32)

    xp = x_prompt.reshape(b * t, D_MODEL)
    xs = x_sample.reshape(nbt * ts, D_MODEL)
    outs_p = [[] for _ in range(6)]
    outs_s = [[] for _ in range(6)]
    for l in range(depth):
        final = l == depth - 1
        q, akv, bkv, ckv, cwin, small = [o.reshape(b, t, -1) for o in
                                         _in_proj(xp, ln1_g[l], w_in_p[l], small_bias[l])]
        ya = _moba_prompt(q, akv, tiles_a, far_a)
        yb = _fox_prompt(q, bkv, small)
        kc, vc = _compress_prompt(ckv, pe2[l], w1bd[l], w2bd[l])
        yc = _nsa_prompt(q, ckv, cwin, jnp.swapaxes(small, 1, 2), tiles_c, far_c, bc_p, cover_t, kc, vc)
        xp, gk = _ffn(xp, ya.reshape(b * t, -1), yb.reshape(b * t, -1), yc.reshape(b * t, -1),
                      woa[l], wob[l], woc[l], ln2_g[l], final_g, wg[l], wu[l], wd[l], conv_w[l], conv_b[l],
                      sample=False, blocks_per_seq=t // ffn_tm, final=final, tm=ffn_tm)
        outs_p[0].append(akv.reshape(b, t, 2, A_HEADS, HEAD_DIM))
        outs_p[1].append(bkv.reshape(b, t, 2, B_HEADS, HEAD_DIM))
        outs_p[2].append(small[..., :B_HEADS])
        outs_p[3].append(ckv.reshape(b, t, 4, C_KV_HEADS, HEAD_DIM))
        outs_p[4].append(cwin[:, t - WINDOW:].reshape(b, WINDOW, 2, C_KV_HEADS, HEAD_DIM))
        outs_p[5].append(gk.reshape(b, t // ffn_tm, SUBLANES, D_FF)[:, -1, SUBLANES - (CONV_W - 1):])
        q, akv, bkv, ckv, cwin, small = [o.reshape(nbt, ts, -1) for o in
                                         _in_proj(xs, ln1_g[l], w_in_p[l], small_bias[l])]
        pt = pt_flat + l * n_pool
        ya = _moba_sample(pt, q, akv, cache_a, *moba_tabs)
        yb = _fox_sample(pt, q, bkv, small, cache_b, cache_lf)
        gl = small[..., B_HEADS:B_HEADS + C_HEADS * N_GATES].reshape(nbt, ts, C_HEADS, N_GATES)
        gl = jnp.swapaxes(gl[:, :, np.asarray(C_SLOT_HEADS)], 1, 2).reshape(nbt, C_HEADS * ts, N_GATES)
        gl = jnp.pad(gl, ((0, 0), (0, 0), (0, LANES - N_GATES)))
        yc = _nsa_sample(pt, q, ckv, cwin, gl, win_all, l, cache_c, nsa_tabs, pe2[l], w1bd[l], w2bd[l])
        prev = state_ffn_conv[l]
        zero = jnp.zeros((nbt, ts - 2, D_FF), F32)
        p1 = jnp.concatenate([prev[:, 1:2], zero, zero[:, :1]], axis=1).reshape(nbt * ts, D_FF)
        p2 = jnp.concatenate([prev, zero], axis=1).reshape(nbt * ts, D_FF)
        xs, g_s = _ffn(xs, ya.reshape(nbt * ts, -1), yb.reshape(nbt * ts, -1), yc.reshape(nbt * ts, -1),
                       woa[l], wob[l], woc[l], ln2_g[l], final_g, wg[l], wu[l], wd[l], conv_w[l], conv_b[l],
                       sample=True, final=final, inj=(p1, p2), tm=ffn_tm)
        outs_s[0].append(akv.reshape(nbt, ts, 2, A_HEADS, HEAD_DIM))
        outs_s[1].append(bkv.reshape(nbt, ts, 2, B_HEADS, HEAD_DIM))
        outs_s[2].append(small[..., :B_HEADS])
        outs_s[3].append(ckv.reshape(nbt, ts, 4, C_KV_HEADS, HEAD_DIM))
        outs_s[4].append(jnp.concatenate([state_c_win[l][:, ts:],
                                          cwin.reshape(nbt, ts, 2, C_KV_HEADS, HEAD_DIM)], axis=1))
        outs_s[5].append(g_s.reshape(nbt, ts, D_FF)[:, ts - (CONV_W - 1):])
    st = lambda rows: jnp.stack(rows, axis=0)
    return (xp.reshape(b, t, D_MODEL), xs.reshape(nbt, ts, D_MODEL),
            *[st(r) for r in outs_p], *[st(r) for r in outs_s])
```

```python
import functools
import math

import numpy as np
import jax
import jax.numpy as jnp
from jax import lax
from jax.experimental import pallas as pl
from jax.experimental.pallas import tpu as pltpu

F32 = jnp.float32
BF16 = jnp.bfloat16
HIGHEST = lax.Precision.HIGHEST

D_MODEL = 1024
HEAD_DIM = 64
A_HEADS = 4
B_HEADS = 6
C_HEADS = 6
C_KV_HEADS = 2
C_GROUP = 3
MOBA_BLOCK = 256
MOBA_TOPK = 3
CMP_BLOCK = 32
CMP_STRIDE = 16
CMP_HIDDEN = 128
SLC_BLOCK = 64
SLC_TOPK = 8
WINDOW = 512
N_GATES = 3
N_BUCKETS = 32
MAX_DISTANCE = 128
D_FF = 2816
CONV_W = 3
RMS_EPS = 1e-6
NEG_INF = -1e30
FORCE_SCORE = 1e6
PAGE_SIZE = 128

LANES = 128
SUBLANES = 8
TQ = 256
VMEM_LIMIT = 56 * 1024 * 1024

C_SLOT_HEADS = (0, 3, 1, 4, 2, 5)

Q_COLS = (A_HEADS + B_HEADS + C_HEADS) * HEAD_DIM
AKV_COLS = 2 * A_HEADS * HEAD_DIM
BKV_COLS = 2 * B_HEADS * HEAD_DIM
CKV_COLS = 4 * C_KV_HEADS * HEAD_DIM
CWIN_COLS = 2 * C_KV_HEADS * HEAD_DIM
SMALL_COLS = LANES
IN_COLS = Q_COLS + AKV_COLS + BKV_COLS + CKV_COLS + CWIN_COLS + SMALL_COLS

_NT = (((1,), (1,)), ((), ()))


def _bucket_table():
    n = np.arange(MAX_DISTANCE + 1)
    exact = N_BUCKETS // 2
    nf = np.maximum(n, 1).astype(np.float32)
    large = exact + (np.log(nf / exact) / math.log(MAX_DISTANCE / exact) * (N_BUCKETS - exact)).astype(np.int32)
    large = np.minimum(large, N_BUCKETS - 1)
    return np.where(n < exact, n, large).astype(np.int32)


_BUCKETS = _bucket_table()


def _cparams(sem, vmem=VMEM_LIMIT):
    return pltpu.CompilerParams(dimension_semantics=sem, vmem_limit_bytes=vmem)


def _gelu(x):
    return 0.5 * x * (1.0 + jnp.tanh(math.sqrt(2.0 / math.pi) * (x + 0.044715 * (x * x * x))))


def _log_sigmoid(x):
    return -(jnp.maximum(-x, 0.0) + jnp.log1p(jnp.exp(-jnp.abs(x))))


def _in_proj_kernel(x_ref, g_ref, w_ref, sb_ref, q_ref, akv_ref, bkv_ref, ckv_ref, cwin_ref, small_ref):
    x = x_ref[...]
    ms = jnp.mean(x * x, axis=-1, keepdims=True)
    h = (x * lax.rsqrt(ms + RMS_EPS)) * g_ref[...]
    hb = h.astype(BF16)
    off = 0
    for ref in (q_ref, akv_ref, bkv_ref, ckv_ref, cwin_ref):
        n = ref.shape[-1]
        ref[...] = jnp.dot(hb, w_ref[:, off:off + n], preferred_element_type=F32)
        off += n
    raw = jnp.dot(hb, w_ref[:, off:off + SMALL_COLS], preferred_element_type=F32) + sb_ref[...]
    lane = lax.broadcasted_iota(jnp.int32, raw.shape, 1)
    small_ref[...] = jnp.where(lane < B_HEADS, _log_sigmoid(raw), raw)


def _in_proj(x2d, g, w, small_bias, tm=256):
    n = x2d.shape[0]
    tm = min(tm, n)
    widths = (Q_COLS, AKV_COLS, BKV_COLS, CKV_COLS, CWIN_COLS, SMALL_COLS)
    return pl.pallas_call(
        _in_proj_kernel,
        grid=(n // tm,),
        in_specs=[pl.BlockSpec((tm, D_MODEL), lambda i: (i, 0)),
                  pl.BlockSpec((1, D_MODEL), lambda i: (0, 0)),
                  pl.BlockSpec((D_MODEL, IN_COLS), lambda i: (0, 0)),
                  pl.BlockSpec((1, SMALL_COLS), lambda i: (0, 0))],
        out_specs=[pl.BlockSpec((tm, c), lambda i: (i, 0)) for c in widths],
        out_shape=[jax.ShapeDtypeStruct((n, c), F32) for c in widths],
        compiler_params=_cparams(("parallel",)),
        name="in_proj",
    )(x2d, g.reshape(1, D_MODEL), w, small_bias)


def _flash_init(s, vt):
    m = jnp.max(s, axis=0, keepdims=True)
    p = jnp.exp(s - m)
    l = jnp.sum(p, axis=0, keepdims=True)
    acc = jnp.dot(vt, p.astype(BF16), preferred_element_type=F32)
    return m, l, acc


def _flash_step(s, vt, m, l, acc):
    m_new = jnp.maximum(m, jnp.max(s, axis=0, keepdims=True))
    a = jnp.exp(m - m_new)
    p = jnp.exp(s - m_new)
    l = a * l + jnp.sum(p, axis=0, keepdims=True)
    acc = a * acc + jnp.dot(vt, p.astype(BF16), preferred_element_type=F32)
    return m_new, l, acc


def _flash_pipeline(n_tiles, score_fn, vt_fn, sbuf, heads=2, first=None):
    for h in range(heads):
        sbuf[h] = score_fn(h, 0) if first is None else first(h)
    init = []
    for h in range(heads):
        init += [jnp.full((1, TQ), NEG_INF, F32), jnp.zeros((1, TQ), F32), jnp.zeros((HEAD_DIM, TQ), F32)]

    def body(i, st):
        cur = [sbuf[h] for h in range(heads)]
        nxt = [score_fn(h, jnp.minimum(i + 1, n_tiles - 1)) for h in range(heads)]
        out = []
        for h in range(heads):
            out.extend(_flash_step(cur[h], vt_fn(h, i), *st[3 * h:3 * h + 3]))
        for h in range(heads):
            sbuf[h] = nxt[h]
        return tuple(out)

    return lax.fori_loop(0, n_tiles, body, tuple(init))


def _top_rank(vals, row):
    rank = jnp.zeros(vals.shape, F32)
    for j in range(vals.shape[0]):
        vj = vals[j:j + 1, :]
        beats = (vj > vals) | ((vj == vals) & (j < row))
        rank = rank + beats.astype(F32)
    return rank


def _head_mask(q, h):
    lane = lax.broadcasted_iota(jnp.int32, (1, LANES), 1)
    return jnp.where((lane // HEAD_DIM) == h, q, 0.0)


def _moba_p_kernel(q_ref, k_ref, v_ref, b3_ref, o_ref, kb, vtb, kmean, sel, sbuf):
    qi = pl.program_id(2)
    nb = kmean.shape[0]

    @pl.when(qi == 0)
    def _prep():
        k = k_ref[0]
        kb[...] = k.astype(BF16)
        vtb[...] = v_ref[0].T.astype(BF16)
        for j in range(nb):
            kmean[j:j + 1, :] = jnp.mean(k[j * TQ:(j + 1) * TQ, :], axis=0, keepdims=True)

    q = q_ref[0]
    rowj = lax.broadcasted_iota(jnp.int32, (nb, TQ), 0)
    qbs = []
    for h in range(2):
        qh = _head_mask(q, h)
        qbs.append(qh.astype(BF16))
        g = lax.dot_general(kmean[...], qh, _NT, precision=HIGHEST, preferred_element_type=F32)
        gm = jnp.where(rowj < qi, g, NEG_INF)
        chosen = ((_top_rank(gm, rowj) < MOBA_TOPK) & (rowj < qi)) | (rowj == qi)
        sel[h] = jnp.where(chosen, 0.0, NEG_INF)

    def score_fn(h, i):
        j = qi - i
        kblk = kb[pl.ds(pl.multiple_of(j * TQ, TQ), TQ), :]
        bias = b3_ref[h, pl.ds(pl.multiple_of(jnp.minimum(i, 2) * TQ, TQ), TQ), :]
        return lax.dot_general(kblk, qbs[h], _NT, preferred_element_type=F32) + bias + sel[h, pl.ds(j, 1), :]

    def vt_fn(h, i):
        return vtb[h * HEAD_DIM:(h + 1) * HEAD_DIM, pl.ds(pl.multiple_of((qi - i) * TQ, TQ), TQ)]

    st = _flash_pipeline(qi + 1, score_fn, vt_fn, sbuf)
    o = jnp.concatenate([st[2] / st[1], st[5] / st[4]], axis=0)
    o_ref[0] = o.T


def _moba_prompt(q_all, a_kv, tiles3):
    b, t, _ = q_all.shape
    npair = A_HEADS // 2
    nb = t // TQ
    return pl.pallas_call(
        _moba_p_kernel,
        grid=(b, npair, nb),
        in_specs=[pl.BlockSpec((1, TQ, LANES), lambda bi, p, qi: (bi, qi, p)),
                  pl.BlockSpec((1, t, LANES), lambda bi, p, qi: (bi, 0, p)),
                  pl.BlockSpec((1, t, LANES), lambda bi, p, qi: (bi, 0, npair + p)),
                  pl.BlockSpec((2, 3 * TQ, TQ), lambda bi, p, qi: (p, 0, 0))],
        out_specs=pl.BlockSpec((1, TQ, LANES), lambda bi, p, qi: (bi, qi, p)),
        out_shape=jax.ShapeDtypeStruct((b, t, A_HEADS * HEAD_DIM), F32),
        scratch_shapes=[pltpu.VMEM((t, LANES), BF16), pltpu.VMEM((LANES, t), BF16),
                        pltpu.VMEM((nb, LANES), F32), pltpu.VMEM((2, nb, TQ), F32),
                        pltpu.VMEM((2, TQ, TQ), F32)],
        compiler_params=_cparams(("parallel", "parallel", "arbitrary")),
        name="moba_prompt",
    )(q_all, a_kv, a_kv, tiles3)


def _fox_p_kernel(q_ref, k_ref, v_ref, lf_ref, o_ref, kb, vtb, ckb, sbuf):
    p = pl.program_id(1)
    qi = pl.program_id(2)
    t = kb.shape[0]
    nb = t // TQ

    @pl.when(qi == 0)
    def _prep():
        kb[...] = k_ref[0].astype(BF16)
        vtb[...] = v_ref[0].T.astype(BF16)
        r = lax.broadcasted_iota(jnp.int32, (TQ, TQ), 0)
        c = lax.broadcasted_iota(jnp.int32, (TQ, TQ), 1)
        tri = (c <= r).astype(BF16)
        sr = lax.broadcasted_iota(jnp.int32, (LANES, LANES), 0)
        parts = _split3(lf_ref[0])
        for h in range(2):
            pick = (sr == 2 * p + h).astype(BF16)
            sel3 = jnp.concatenate([jnp.dot(x, pick, preferred_element_type=F32).astype(BF16) for x in parts],
                                   axis=1)
            carry = jnp.zeros((1, LANES), F32)
            for j in range(nb):
                c3 = jnp.dot(tri, sel3[j * TQ:(j + 1) * TQ, :], preferred_element_type=F32)
                blk = c3[:, 0:LANES] + c3[:, LANES:2 * LANES] + c3[:, 2 * LANES:3 * LANES] + carry
                ckb[h, j * TQ:(j + 1) * TQ, :] = blk
                carry = blk[TQ - 1:TQ, :]

    q = q_ref[0]
    qbs = [_head_mask(q, h).astype(BF16) for h in range(2)]
    kk = lax.broadcasted_iota(jnp.int32, (TQ, LANES), 0)
    for h in range(2):
        own = ckb[h, pl.ds(pl.multiple_of(qi * TQ, TQ), TQ), :]
        for half in range(TQ // LANES):
            qq = lax.broadcasted_iota(jnp.int32, (TQ, LANES), 1) + half * LANES
            ckb[h, t + half * TQ:t + (half + 1) * TQ, :] = jnp.where(kk <= qq, own, -NEG_INF)

    def score_fn(h, i):
        j = qi - i
        kblk = kb[pl.ds(pl.multiple_of(j * TQ, TQ), TQ), :]
        s = lax.dot_general(kblk, qbs[h], _NT, preferred_element_type=F32)
        c = ckb[h, pl.ds(pl.multiple_of(j * TQ, TQ), TQ), :]
        return s - jnp.concatenate([c, c], axis=1)

    def score_own(h):
        kblk = kb[pl.ds(pl.multiple_of(qi * TQ, TQ), TQ), :]
        s = lax.dot_general(kblk, qbs[h], _NT, preferred_element_type=F32)
        return s - jnp.concatenate([ckb[h, t:t + TQ, :], ckb[h, t + TQ:t + 2 * TQ, :]], axis=1)

    def vt_fn(h, i):
        return vtb[h * HEAD_DIM:(h + 1) * HEAD_DIM, pl.ds(pl.multiple_of((qi - i) * TQ, TQ), TQ)]

    st = _flash_pipeline(qi + 1, lambda h, i: score_fn(h, i), vt_fn, sbuf, first=score_own)
    o = jnp.concatenate([st[2] / st[1], st[5] / st[4]], axis=0)
    o_ref[0] = o.T


def _fox_prompt(q_all, b_kv, small):
    b, t, _ = q_all.shape
    npair = B_HEADS // 2
    qoff = A_HEADS // 2
    nb = t // TQ
    return pl.pallas_call(
        _fox_p_kernel,
        grid=(b, npair, nb),
        in_specs=[pl.BlockSpec((1, TQ, LANES), lambda bi, p, qi: (bi, qi, qoff + p)),
                  pl.BlockSpec((1, t, LANES), lambda bi, p, qi: (bi, 0, p)),
                  pl.BlockSpec((1, t, LANES), lambda bi, p, qi: (bi, 0, npair + p)),
                  pl.BlockSpec((1, t, LANES), lambda bi, p, qi: (bi, 0, 0))],
        out_specs=pl.BlockSpec((1, TQ, LANES), lambda bi, p, qi: (bi, qi, p)),
        out_shape=jax.ShapeDtypeStruct((b, t, B_HEADS * HEAD_DIM), F32),
        scratch_shapes=[pltpu.VMEM((t, LANES), BF16), pltpu.VMEM((LANES, t), BF16),
                        pltpu.VMEM((2, t + 2 * TQ, LANES), F32), pltpu.VMEM((2, TQ, TQ), F32)],
        compiler_params=_cparams(("parallel", "parallel", "arbitrary")),
        name="fox_prompt",
    )(q_all, b_kv, b_kv, small)


def _compress_pair(rows_of, ncp, pe_ref, w1_ref, w2_ref, kind):
    half = CMP_BLOCK // 2
    acc_a = jnp.zeros((ncp, 2 * CMP_HIDDEN), F32)
    acc_b = jnp.zeros((ncp, 2 * CMP_HIDDEN), F32)
    for l in range(half):
        r = rows_of(l)
        xa = (r + pe_ref[kind, l:l + 1, :]).astype(BF16)
        xb = (r + pe_ref[kind, half + l:half + l + 1, :]).astype(BF16)
        acc_a = acc_a + jnp.dot(xa, w1_ref[kind, l], preferred_element_type=F32)
        acc_b = acc_b + jnp.dot(xb, w1_ref[kind, half + l], preferred_element_type=F32)
    pre = acc_a + pltpu.roll(acc_b, ncp - 1, 0)
    hdn = _gelu(pre).astype(BF16)
    return jnp.dot(hdn, w2_ref[kind], preferred_element_type=F32)


def _compress_p_kernel(ck_ref, cv_ref, pe_ref, w1_ref, w2_ref, kc_ref, vc_ref):
    ncp = kc_ref.shape[1]

    def rows_of(ref):
        return lambda l: ref[0, pl.ds(l, ncp, stride=CMP_STRIDE), :]

    kc_ref[0] = _compress_pair(rows_of(ck_ref), ncp, pe_ref, w1_ref, w2_ref, 0).astype(BF16)
    vc_ref[0] = _compress_pair(rows_of(cv_ref), ncp, pe_ref, w1_ref, w2_ref, 1)


def _compress_prompt(c_kv, pe2, w1bd, w2bd):
    b, t, _ = c_kv.shape
    ncp = t // CMP_STRIDE
    const = lambda *shape: pl.BlockSpec(shape, lambda bi: (0,) * len(shape))
    return pl.pallas_call(
        _compress_p_kernel,
        grid=(b,),
        in_specs=[pl.BlockSpec((1, t, LANES), lambda bi: (bi, 0, 0)),
                  pl.BlockSpec((1, t, LANES), lambda bi: (bi, 0, 1)),
                  const(2, CMP_BLOCK, LANES), const(2, CMP_BLOCK, LANES, 2 * CMP_HIDDEN),
                  const(2, 2 * CMP_HIDDEN, LANES)],
        out_specs=[pl.BlockSpec((1, ncp, LANES), lambda bi: (bi, 0, 0)),
                   pl.BlockSpec((1, ncp, LANES), lambda bi: (bi, 0, 0))],
        out_shape=[jax.ShapeDtypeStruct((b, ncp, LANES), BF16), jax.ShapeDtypeStruct((b, ncp, LANES), F32)],
        compiler_params=_cparams(("parallel",)),
        name="compress_prompt",
    )(c_kv, c_kv, pe2, w1bd, w2bd)


def _nsa_p_kernel(q0_ref, q1_ref, q2_ref, ckv_ref, cwin_ref, gt_ref, t3_ref, w5_ref, bc_ref, cov_ref,
                  kcb, vc_ref, o_ref, vctb, skb, svtb, wkb, wvtb, selsc, sbuf):
    qi = pl.program_id(1)
    t = skb.shape[0]
    ncp = kcb.shape[1]
    nbs = selsc.shape[1]
    per_tile = TQ // SLC_BLOCK

    @pl.when(qi == 0)
    def _prep():
        vctb[...] = vc_ref[0].T.astype(BF16)
        skb[...] = ckv_ref[0, :, 0:LANES].astype(BF16)
        svtb[...] = ckv_ref[0, :, LANES:2 * LANES].T.astype(BF16)
        wkb[...] = cwin_ref[0, :, 0:LANES].astype(BF16)
        wvtb[...] = cwin_ref[0, :, LANES:2 * LANES].T.astype(BF16)

    q_refs = (q0_ref, q1_ref, q2_ref)
    qpos = qi * TQ + lax.broadcasted_iota(jnp.int32, (1, TQ), 1)
    ci = lax.broadcasted_iota(jnp.int32, (ncp, TQ), 0)
    cok = (qpos >= ci * CMP_STRIDE + (CMP_BLOCK - 1)) & (ci < ncp - 1)

    qbs, ocmp = [], []
    psum = [jnp.zeros((ncp, TQ), F32), jnp.zeros((ncp, TQ), F32)]
    for s in range(C_HEADS):
        kvh = s % 2
        qb = _head_mask(q_refs[s // 2][0], kvh).astype(BF16)
        qbs.append(qb)
        band0 = pl.multiple_of(ncp - qi * (TQ // CMP_STRIDE), TQ // CMP_STRIDE)
        sc = lax.dot_general(kcb[0], qb, _NT, preferred_element_type=F32) + bc_ref[s, pl.ds(band0, ncp), :]
        sc = jnp.where(cok, sc, NEG_INF)
        m = jnp.max(sc, axis=0, keepdims=True)
        e = jnp.where(cok, jnp.exp(sc - m), 0.0)
        den = jnp.sum(e, axis=0, keepdims=True)
        pc = e * jnp.where(den > 0.0, 1.0 / den, 0.0)
        psum[kvh] = psum[kvh] + pc
        ocmp.append(jnp.dot(vctb[kvh * HEAD_DIM:(kvh + 1) * HEAD_DIM, :], pc.astype(BF16),
                            preferred_element_type=F32))

    jj = lax.broadcasted_iota(jnp.int32, (nbs, TQ), 0)
    own = qpos // SLC_BLOCK
    forced = (jj == 0) | (jj == own) | (jj == own - 1)
    for kvh in range(2):
        imp = jnp.dot(cov_ref[...], psum[kvh], precision=HIGHEST, preferred_element_type=F32)
        v = jnp.where(jj <= own, jnp.where(forced, FORCE_SCORE, imp), NEG_INF)
        chosen = (_top_rank(v, jj) < SLC_TOPK) & (jj <= own)
        selsc[kvh] = jnp.where(chosen, 0.0, NEG_INF)

    def sel_rows(kvh, j):
        rows = [jnp.broadcast_to(selsc[kvh, pl.ds(j * per_tile + r, 1), :], (SLC_BLOCK, TQ))
                for r in range(per_tile)]
        return jnp.concatenate(rows, axis=0)

    wn = WINDOW + TQ
    wstart = pl.multiple_of(jnp.maximum(qi - WINDOW // TQ, 0) * TQ, TQ)
    woff = pl.multiple_of((WINDOW // TQ - jnp.minimum(qi, WINDOW // TQ)) * TQ, TQ)

    def score_fn(s, i):
        j = qi - i
        kblk = skb[pl.ds(pl.multiple_of(j * TQ, TQ), TQ), :]
        bias = t3_ref[s, pl.ds(pl.multiple_of(jnp.minimum(i, 2) * TQ, TQ), TQ), :]
        return lax.dot_general(kblk, qbs[s], _NT, preferred_element_type=F32) + bias + sel_rows(s % 2, j)

    def vt_fn(s, i):
        kvh = s % 2
        return svtb[kvh * HEAD_DIM:(kvh + 1) * HEAD_DIM, pl.ds(pl.multiple_of((qi - i) * TQ, TQ), TQ)]

    st_slc = _flash_pipeline(qi + 1, score_fn, vt_fn, sbuf, heads=C_HEADS)

    for pair in range(C_HEADS // 2):
        outs = []
        for s in (2 * pair, 2 * pair + 1):
            n = s
            kvh = s % 2
            sc = lax.dot_general(wkb[pl.ds(wstart, wn), :], qbs[s], _NT, preferred_element_type=F32)
            sc = sc + w5_ref[s, pl.ds(woff, wn), :]
            m, l, acc = _flash_init(sc, wvtb[kvh * HEAD_DIM:(kvh + 1) * HEAD_DIM, pl.ds(wstart, wn)])
            o_win = acc / l
            o_slc = st_slc[3 * n + 2] / st_slc[3 * n + 1]
            grow = B_HEADS + C_SLOT_HEADS[s] * N_GATES
            g = [jax.nn.sigmoid(gt_ref[0, grow + n_:grow + n_ + 1, :]) for n_ in range(N_GATES)]
            outs.append(g[0] * ocmp[s] + g[1] * o_slc + g[2] * o_win)
        o_ref[0, :, pair * LANES:(pair + 1) * LANES] = jnp.concatenate(outs, axis=0).T


def _nsa_prompt(q_all, c_kv, c_win, small_t, tiles3, tiles5, bc, cover_t, kc, vct):
    b, t, _ = q_all.shape
    assert t >= WINDOW + TQ
    nq = t // TQ
    ncp = t // CMP_STRIDE
    nbs = t // SLC_BLOCK
    qoff = (A_HEADS + B_HEADS) // 2
    const = lambda *shape: pl.BlockSpec(shape, lambda bi, qi: (0,) * len(shape))
    qspec = lambda m: pl.BlockSpec((1, TQ, LANES), lambda bi, qi: (bi, qi, qoff + m))
    return pl.pallas_call(
        _nsa_p_kernel,
        grid=(b, nq),
        in_specs=[qspec(0), qspec(1), qspec(2),
                  pl.BlockSpec((1, t, 2 * LANES), lambda bi, qi: (bi, 0, 1)),
                  pl.BlockSpec((1, t, CWIN_COLS), lambda bi, qi: (bi, 0, 0)),
                  pl.BlockSpec((1, SMALL_COLS, TQ), lambda bi, qi: (bi, 0, qi)),
                  const(C_HEADS, 3 * TQ, TQ),
                  const(C_HEADS, WINDOW + 3 * TQ, TQ),
                  const(C_HEADS, 2 * ncp, TQ),
                  const(nbs, ncp),
                  pl.BlockSpec((1, ncp, LANES), lambda bi, qi: (bi, 0, 0)),
                  pl.BlockSpec((1, ncp, LANES), lambda bi, qi: (bi, 0, 0))],
        out_specs=pl.BlockSpec((1, TQ, C_HEADS * HEAD_DIM), lambda bi, qi: (bi, qi, 0)),
        out_shape=jax.ShapeDtypeStruct((b, t, C_HEADS * HEAD_DIM), F32),
        scratch_shapes=[pltpu.VMEM((LANES, ncp), BF16),
                        pltpu.VMEM((t, LANES), BF16), pltpu.VMEM((LANES, t), BF16),
                        pltpu.VMEM((t, LANES), BF16), pltpu.VMEM((LANES, t), BF16),
                        pltpu.VMEM((2, nbs, TQ), F32), pltpu.VMEM((C_HEADS, TQ, TQ), F32)],
        compiler_params=_cparams(("parallel", "arbitrary")),
        name="nsa_prompt",
    )(q_all, q_all, q_all, c_kv, c_win, small_t, tiles3, tiles5, bc, cover_t, kc, vct)


FF_CHUNK = 256


def _rms(x, g):
    ms = jnp.mean(x * x, axis=-1, keepdims=True)
    return (x * lax.rsqrt(ms + RMS_EPS)) * g


def _ffn_kernel(sample, blocks_per_seq, final, x_ref, ya_ref, yb_ref, yc_ref, woa_ref, wob_ref, woc_ref,
                g2_ref, gf_ref, wg_ref, wu_ref, wd_ref, cw_ref, cb_ref, *rest):
    if sample:
        p1_ref, p2_ref, o_ref, gk_ref, x1_s, h2_s, acc_s = rest
    else:
        o_ref, gk_ref, x1_s, h2_s, acc_s, halo_s = rest
    i = pl.program_id(0)
    c = pl.program_id(1)
    tm = x_ref.shape[0]

    @pl.when(c == 0)
    def _attn_out():
        y = (jnp.dot(ya_ref[...].astype(BF16), woa_ref[...], preferred_element_type=F32)
             + jnp.dot(yb_ref[...].astype(BF16), wob_ref[...], preferred_element_type=F32)
             + jnp.dot(yc_ref[...].astype(BF16), woc_ref[...], preferred_element_type=F32))
        x1 = x_ref[...] + y
        x1_s[...] = x1
        h2_s[...] = _rms(x1, g2_ref[...]).astype(BF16)
        acc_s[...] = jnp.zeros(acc_s.shape, F32)

    hb = h2_s[...]
    g = jnp.dot(hb, wg_ref[...], preferred_element_type=F32)
    u = jnp.dot(hb, wu_ref[...], preferred_element_type=F32)
    row = lax.broadcasted_iota(jnp.int32, g.shape, 0)
    r1 = pltpu.roll(g, 1, 0)
    r2 = pltpu.roll(g, 2, 0)
    if sample:
        t = row % SUBLANES
        gs1 = jnp.where(t >= 1, r1, p1_ref[...])
        gs2 = jnp.where(t >= 2, r2, p2_ref[...])
        gk_ref[...] = g
    else:
        halo = jnp.where(i % blocks_per_seq == 0, 0.0, halo_s[c])
        gs1 = jnp.where(row >= 1, r1, halo[SUBLANES - 1:SUBLANES, :])
        gs2 = jnp.where(row >= 2, r2, jnp.where(row == 0, halo[SUBLANES - 2:SUBLANES - 1, :],
                                                  halo[SUBLANES - 1:SUBLANES, :]))
        tail = g[tm - SUBLANES:tm, :]
        halo_s[c] = tail
        gk_ref[0] = tail
    conv = cb_ref[...] + cw_ref[0:1, :] * gs2
    conv = conv + cw_ref[1:2, :] * gs1
    conv = conv + cw_ref[2:3, :] * g
    act = (_gelu(conv) * u).astype(BF16)
    acc_s[...] += jnp.dot(act, wd_ref[...], preferred_element_type=F32)

    @pl.when(c == pl.num_programs(1) - 1)
    def _finish():
        x2 = x1_s[...] + acc_s[...]
        o_ref[...] = _rms(x2, gf_ref[...]) if final else x2


def _ffn(x2d, ya, yb, yc, woa, wob, woc, g2, gf, wg, wu, wd, cw, cb, *, sample, blocks_per_seq=1, final=False,
         inj=None, tm=512):
    n = x2d.shape[0]
    tm = min(tm, n)
    nc = D_FF // FF_CHUNK
    rows = lambda w: pl.BlockSpec((tm, w), lambda i, c: (i, 0))
    const = lambda *shape: pl.BlockSpec(shape, lambda i, c: (0,) * len(shape))
    in_specs = [rows(D_MODEL), rows(ya.shape[1]), rows(yb.shape[1]), rows(yc.shape[1]),
                const(*woa.shape), const(*wob.shape), const(*woc.shape), const(1, D_MODEL), const(1, D_MODEL),
                pl.BlockSpec((D_MODEL, FF_CHUNK), lambda i, c: (0, c)),
                pl.BlockSpec((D_MODEL, FF_CHUNK), lambda i, c: (0, c)),
                pl.BlockSpec((FF_CHUNK, D_MODEL), lambda i, c: (c, 0)),
                pl.BlockSpec((CONV_W, FF_CHUNK), lambda i, c: (0, c)),
                pl.BlockSpec((1, FF_CHUNK), lambda i, c: (0, c))]
    args = [x2d, ya, yb, yc, woa, wob, woc, g2.reshape(1, D_MODEL), gf.reshape(1, D_MODEL), wg, wu, wd, cw,
            cb.reshape(1, D_FF)]
    scratch = [pltpu.VMEM((tm, D_MODEL), F32), pltpu.VMEM((tm, D_MODEL), BF16), pltpu.VMEM((tm, D_MODEL), F32)]
    if sample:
        in_specs += [pl.BlockSpec((tm, FF_CHUNK), lambda i, c: (i, c))] * 2
        args += list(inj)
        gk_spec = pl.BlockSpec((tm, FF_CHUNK), lambda i, c: (i, c))
        gk_shape = jax.ShapeDtypeStruct((n, D_FF), F32)
    else:
        scratch.append(pltpu.VMEM((nc, SUBLANES, FF_CHUNK), F32))
        gk_spec = pl.BlockSpec((1, SUBLANES, FF_CHUNK), lambda i, c: (i, 0, c))
        gk_shape = jax.ShapeDtypeStruct((n // tm, SUBLANES, D_FF), F32)
    return pl.pallas_call(
        functools.partial(_ffn_kernel, sample, blocks_per_seq, final),
        grid=(n // tm, nc),
        in_specs=in_specs,
        out_specs=[pl.BlockSpec((tm, D_MODEL), lambda i, c: (i, 0)), gk_spec],
        out_shape=[jax.ShapeDtypeStruct((n, D_MODEL), F32), gk_shape],
        scratch_shapes=scratch,
        compiler_params=_cparams(("arbitrary", "arbitrary")),
        name="ffn_sample" if sample else "ffn_prompt",
    )(*args)


N_PAGES = 16
PAST_LEN = N_PAGES * PAGE_SIZE
DEC_SEQ = SUBLANES


def _top_rank_lanes(vals, n):
    lane = lax.broadcasted_iota(jnp.int32, vals.shape, 1)
    rank = jnp.zeros(vals.shape, F32)
    for j in range(n):
        vj = vals[:, j:j + 1]
        beats = (vj > vals) | ((vj == vals) & (j < lane))
        rank = rank + beats.astype(F32)
    return rank


def _stack_heads(q, lane_heads):
    return jnp.concatenate([_head_mask(q, h) for h in lane_heads], axis=0)


def _pad_rows(x, rows=LANES):
    return jnp.concatenate([x, jnp.zeros((rows - x.shape[0], x.shape[1]), x.dtype)], axis=0)


def _softmax_pv(parts):
    m = parts[0][0].max(axis=1, keepdims=True)
    for s, _, _ in parts[1:]:
        m = jnp.maximum(m, s.max(axis=1, keepdims=True))
    l = 0.0
    o = 0.0
    for s, v, feature_major in parts:
        p = jnp.exp(s - m)
        l = l + jnp.sum(p, axis=1, keepdims=True)
        if feature_major:
            o = o + lax.dot_general(p.astype(BF16), v, _NT, preferred_element_type=F32)
        else:
            o = o + jnp.dot(p.astype(BF16), v, preferred_element_type=F32)
    return o / l


def _lane_cat(pages, rows):
    return jnp.concatenate([pg[0, rows, :] for pg in pages], axis=1)


def _pair_out(o, r0, r1):
    lane = lax.broadcasted_iota(jnp.int32, (DEC_SEQ, LANES), 1)
    return jnp.where(lane < HEAD_DIM, o[r0:r0 + DEC_SEQ], o[r1:r1 + DEC_SEQ])


def _moba_s_kernel(pt_ref, q_ref, new_ref, bp_ref, bn_ref, *rest):
    pages = rest[:N_PAGES]
    o_ref = rest[N_PAGES]
    nblk = PAST_LEN // MOBA_BLOCK
    ppb = MOBA_BLOCK // PAGE_SIZE
    width = A_HEADS * HEAD_DIM
    q = q_ref[0]
    lane = lax.broadcasted_iota(jnp.int32, (2 * DEC_SEQ, LANES), 1)
    t_row = lax.broadcasted_iota(jnp.int32, (2 * DEC_SEQ, LANES), 0) % DEC_SEQ
    lane_sq = lax.broadcasted_iota(jnp.int32, (LANES, LANES), 1)
    outs = []
    for p in range(A_HEADS // 2):
        rows = slice(p * LANES, (p + 1) * LANES)
        cols = slice(p * LANES, (p + 1) * LANES)
        kmean_t = jnp.zeros((LANES, LANES), F32)
        for j in range(nblk):
            tot = 0.0
            for pg in pages[j * ppb:(j + 1) * ppb]:
                tot = tot + jnp.sum(pg[0, rows, :], axis=1, keepdims=True)
            kmean_t = jnp.where(lane_sq == j, tot / MOBA_BLOCK, kmean_t)
        q2 = _stack_heads(q[:, cols], (0, 1))
        gate = jnp.dot(q2, kmean_t, precision=HIGHEST, preferred_element_type=F32)
        gm = jnp.where(lane < nblk, gate, NEG_INF)
        chosen = (_top_rank_lanes(gm, nblk) < MOBA_TOPK) & (lane < nblk)
        neg = jnp.where(chosen, 0.0, NEG_INF)
        q2b = q2.astype(BF16)
        ktb = _lane_cat(pages, rows).astype(BF16)
        vtb = _lane_cat(pages, slice(width + p * LANES, width + (p + 1) * LANES)).astype(BF16)
        s_past = jnp.dot(q2b, ktb, preferred_element_type=F32) + bp_ref[p]
        s_past = s_past + jnp.concatenate([jnp.broadcast_to(neg[:, j:j + 1], (2 * DEC_SEQ, MOBA_BLOCK))
                                           for j in range(nblk)], axis=1)
        knp = _pad_rows(new_ref[0, :, cols]).astype(BF16)
        vnp = _pad_rows(new_ref[0, :, width + p * LANES:width + (p + 1) * LANES]).astype(BF16)
        s_new = lax.dot_general(q2b, knp, _NT, preferred_element_type=F32) + bn_ref[p]
        s_new = jnp.where(lane <= t_row, s_new, NEG_INF)
        o = _softmax_pv([(s_past, vtb, True), (s_new, vnp, False)])
        outs.append(_pair_out(o, 0, DEC_SEQ))
    o_ref[0] = jnp.concatenate(outs, axis=1)


def _page_specs(rows):
    def spec(i):
        return pl.BlockSpec((1, rows, PAGE_SIZE), lambda b, pt, *_: (pt[b * N_PAGES + i], 0, 0))
    return [spec(i) for i in range(N_PAGES)]


def _moba_sample(pt, q_s, a_new, cache, bias_past, bias_new):
    nbt = q_s.shape[0]
    row = lambda w: pl.BlockSpec((1, DEC_SEQ, w), lambda b, pt: (b, 0, 0))
    const = lambda *shape: pl.BlockSpec(shape, lambda b, pt: (0,) * len(shape))
    gs = pltpu.PrefetchScalarGridSpec(
        num_scalar_prefetch=1, grid=(nbt,),
        in_specs=[row(Q_COLS), row(AKV_COLS), const(*bias_past.shape), const(*bias_new.shape)]
        + _page_specs(AKV_COLS),
        out_specs=row(A_HEADS * HEAD_DIM))
    return pl.pallas_call(
        _moba_s_kernel, grid_spec=gs,
        out_shape=jax.ShapeDtypeStruct((nbt, DEC_SEQ, A_HEADS * HEAD_DIM), F32),
        compiler_params=_cparams(("arbitrary",)), name="moba_sample",
    )(pt, q_s, a_new, bias_past, bias_new, *([cache] * N_PAGES))


def _split3(x):
    hi = x.astype(BF16)
    r1 = x - hi.astype(F32)
    mid = r1.astype(BF16)
    lo = (r1 - mid.astype(F32)).astype(BF16)
    return hi, mid, lo


def _fox_s_kernel(pt_ref, raw_ref, q_ref, new_ref, small_ref, cm_ref, *rest):
    pages = rest[:N_PAGES]
    lpages = rest[N_PAGES:2 * N_PAGES]
    o_ref = rest[2 * N_PAGES]
    x_s = rest[2 * N_PAGES + 1]
    b = pl.program_id(0)
    width = B_HEADS * HEAD_DIM
    qoff = A_HEADS * HEAD_DIM
    r16 = 2 * DEC_SEQ
    lane = lax.broadcasted_iota(jnp.int32, (r16, LANES), 1)
    t_row = lax.broadcasted_iota(jnp.int32, (r16, LANES), 0) % DEC_SEQ

    x_s[...] = jnp.zeros(x_s.shape, F32)
    for pg in range(N_PAGES):
        r = raw_ref[b * N_PAGES + pg] % SUBLANES
        for h in range(B_HEADS):
            x_s[h * N_PAGES + pg:h * N_PAGES + pg + 1, :] = lpages[pg][h, pl.ds(r, 1), :]
    hi, mid, lo = _split3(x_s[...])
    parts = jnp.dot(jnp.concatenate([hi, mid, lo], axis=0), cm_ref[0], preferred_element_type=F32)
    sfx = parts[0:LANES] + parts[LANES:2 * LANES] + parts[2 * LANES:3 * LANES]
    t_hi, t_mid, t_lo = _split3(sfx[:, LANES:])
    offp = jnp.dot(cm_ref[1, :, 0:LANES], jnp.concatenate([t_hi, t_mid, t_lo], axis=1),
                   preferred_element_type=F32)
    arow = sfx[:, 0:LANES] + (offp[:, 0:LANES] + offp[:, LANES:2 * LANES] + offp[:, 2 * LANES:3 * LANES])

    def decay_row(h):
        return jnp.concatenate([arow[h * N_PAGES + pg:h * N_PAGES + pg + 1, :] for pg in range(N_PAGES)], axis=1)

    lfn = small_ref[0]
    sub = lax.broadcasted_iota(jnp.int32, (DEC_SEQ, LANES), 0)
    lane8 = lax.broadcasted_iota(jnp.int32, (DEC_SEQ, LANES), 1)
    cs = lfn
    for sh in (1, 2, 4):
        cs = cs + jnp.where(sub >= sh, pltpu.roll(cs, sh, 0), 0.0)

    q = q_ref[0]
    outs = []
    for p in range(B_HEADS // 2):
        cols = slice(p * LANES, (p + 1) * LANES)
        q2b = _stack_heads(q[:, qoff + p * LANES:qoff + (p + 1) * LANES], (0, 1)).astype(BF16)
        heads = (2 * p, 2 * p + 1)
        a_rows = jnp.concatenate([jnp.broadcast_to(decay_row(h), (DEC_SEQ, PAST_LEN)) for h in heads], axis=0)
        d_col = jnp.concatenate([cs[:, h:h + 1] for h in heads], axis=0)
        d_row = jnp.concatenate(
            [jnp.broadcast_to(jnp.sum(jnp.where(sub <= lane8, jnp.broadcast_to(lfn[:, h:h + 1], (DEC_SEQ, LANES)),
                                                0.0), axis=0, keepdims=True), (DEC_SEQ, LANES))
             for h in heads], axis=0)
        ktb = _lane_cat(pages, cols).astype(BF16)
        vtb = _lane_cat(pages, slice(width + p * LANES, width + (p + 1) * LANES)).astype(BF16)
        s_past = jnp.dot(q2b, ktb, preferred_element_type=F32) + (a_rows + d_col)
        knp = _pad_rows(new_ref[0, :, cols]).astype(BF16)
        vnp = _pad_rows(new_ref[0, :, width + p * LANES:width + (p + 1) * LANES]).astype(BF16)
        s_new = lax.dot_general(q2b, knp, _NT, preferred_element_type=F32) + (d_col - d_row)
        s_new = jnp.where(lane <= t_row, s_new, NEG_INF)
        o = _softmax_pv([(s_past, vtb, True), (s_new, vnp, False)])
        outs.append(_pair_out(o, 0, DEC_SEQ))
    o_ref[0] = jnp.concatenate(outs, axis=1)


def _fox_consts():
    tok = np.arange(PAGE_SIZE)
    m = np.concatenate([tok[:, None] > tok[None, :], np.ones((PAGE_SIZE, PAGE_SIZE), bool)], axis=1)
    r = np.arange(LANES)
    u = (r[:, None] // N_PAGES == r[None, :] // N_PAGES) & (r[None, :] % N_PAGES > r[:, None] % N_PAGES)
    u = np.concatenate([u, np.zeros((LANES, LANES), bool)], axis=1)
    return jnp.asarray(np.stack([m, u]), BF16)


def _fox_sample(pt, pt_raw, q_s, b_new, small_s, cache, cache_logf, layer):
    nbt = q_s.shape[0]
    row = lambda w: pl.BlockSpec((1, DEC_SEQ, w), lambda b, pt, raw: (b, 0, 0))
    cm = _fox_consts()
    lspecs = [pl.BlockSpec((B_HEADS, SUBLANES, PAGE_SIZE),
                           functools.partial(lambda i, b, pt, raw: (layer, raw[b * N_PAGES + i] // SUBLANES, 0), i))
              for i in range(N_PAGES)]
    gs = pltpu.PrefetchScalarGridSpec(
        num_scalar_prefetch=2, grid=(nbt,),
        in_specs=[row(Q_COLS), row(BKV_COLS), row(SMALL_COLS),
                  pl.BlockSpec(cm.shape, lambda b, pt, raw: (0, 0, 0))] + _page_specs(BKV_COLS) + lspecs,
        out_specs=row(B_HEADS * HEAD_DIM),
        scratch_shapes=[pltpu.VMEM((LANES, PAGE_SIZE), F32)])
    return pl.pallas_call(
        _fox_s_kernel, grid_spec=gs,
        out_shape=jax.ShapeDtypeStruct((nbt, DEC_SEQ, B_HEADS * HEAD_DIM), F32),
        compiler_params=_cparams(("arbitrary",)), name="fox_sample",
    )(pt, pt_raw, q_s, b_new, small_s, cm, *([cache] * N_PAGES), *([cache_logf] * N_PAGES))


def _nsa_s_kernel(pt_ref, q_ref, new_ref, wnew_ref, gate_ref, win_ref, bc_ref, bp_ref, bn_ref, bw_ref,
                  cov_ref, exp_ref, pe_ref, w1_ref, w2_ref, *rest):
    pages = rest[:N_PAGES]
    o_ref = rest[N_PAGES]
    ctok = rest[N_PAGES + 1]
    ncp = PAST_LEN // CMP_STRIDE
    nbs_past = PAST_LEN // SLC_BLOCK
    qoff = (A_HEADS + B_HEADS) * HEAD_DIM
    nrow = C_HEADS * DEC_SEQ
    lane = lax.broadcasted_iota(jnp.int32, (nrow, LANES), 1)
    t_row = lax.broadcasted_iota(jnp.int32, (nrow, LANES), 0) % DEC_SEQ

    for i, pg in enumerate(pages):
        for kind in range(2):
            ctok[kind, i * PAGE_SIZE:(i + 1) * PAGE_SIZE, :] = pg[0, kind * LANES:(kind + 1) * LANES, :].T

    def rows_of(kind):
        return lambda l: ctok[kind, pl.ds(l, ncp, stride=CMP_STRIDE), :]

    kc = _compress_pair(rows_of(0), ncp, pe_ref, w1_ref, w2_ref, 0).astype(BF16)
    vc = _compress_pair(rows_of(1), ncp, pe_ref, w1_ref, w2_ref, 1).astype(BF16)

    q = q_ref[0]
    q2 = jnp.concatenate([_head_mask(q[:, qoff + (s // 2) * LANES:qoff + (s // 2 + 1) * LANES], s % 2)
                          for s in range(C_HEADS)], axis=0)
    q2b = q2.astype(BF16)

    sc = lax.dot_general(q2b, kc, _NT, preferred_element_type=F32) + bc_ref[...]
    sc = jnp.where(lane < ncp - 1, sc, NEG_INF)
    m = sc.max(axis=1, keepdims=True)
    e = jnp.exp(sc - m)
    pc = e / jnp.sum(e, axis=1, keepdims=True)
    o_cmp = jnp.dot(pc.astype(BF16), vc, preferred_element_type=F32)

    psum = [sum(pc[s * DEC_SEQ:(s + 1) * DEC_SEQ] for s in range(kvh, C_HEADS, 2)) for kvh in range(2)]
    imp = jnp.dot(jnp.concatenate(psum, axis=0), cov_ref[...], precision=HIGHEST, preferred_element_type=F32)
    lane16 = lax.broadcasted_iota(jnp.int32, (2 * DEC_SEQ, LANES), 1)
    forced = (lane16 == 0) | (lane16 == nbs_past) | (lane16 == nbs_past - 1)
    v = jnp.where(lane16 <= nbs_past, jnp.where(forced, FORCE_SCORE, imp), NEG_INF)
    chosen = (_top_rank_lanes(v, nbs_past + 1) < SLC_TOPK) & (lane16 <= nbs_past)
    chosen_f = jnp.where(chosen, 1.0, 0.0)
    mask16 = jnp.dot(chosen_f.astype(BF16), exp_ref[...], preferred_element_type=F32)
    mask_past = jnp.concatenate([mask16[(s % 2) * DEC_SEQ:(s % 2 + 1) * DEC_SEQ] for s in range(C_HEADS)], axis=0)
    mask_new = jnp.concatenate(
        [jnp.broadcast_to(chosen_f[(s % 2) * DEC_SEQ:(s % 2 + 1) * DEC_SEQ, nbs_past:nbs_past + 1],
                          (DEC_SEQ, LANES)) for s in range(C_HEADS)], axis=0)

    ktb = _lane_cat(pages, slice(2 * LANES, 3 * LANES)).astype(BF16)
    vtb = _lane_cat(pages, slice(3 * LANES, 4 * LANES)).astype(BF16)
    s_past = jnp.dot(q2b, ktb, preferred_element_type=F32) + bp_ref[...]
    s_past = jnp.where(mask_past > 0.5, s_past, NEG_INF)
    knp = _pad_rows(new_ref[0, :, 2 * LANES:3 * LANES]).astype(BF16)
    vnp = _pad_rows(new_ref[0, :, 3 * LANES:4 * LANES]).astype(BF16)
    s_new = lax.dot_general(q2b, knp, _NT, preferred_element_type=F32) + bn_ref[...]
    s_new = jnp.where((lane <= t_row) & (mask_new > 0.5), s_new, NEG_INF)
    o_slc = _softmax_pv([(s_past, vtb, True), (s_new, vnp, False)])

    wkt = win_ref[0, 0:LANES, :].astype(BF16)
    wvt = win_ref[0, LANES:2 * LANES, :].astype(BF16)
    wlane = lax.broadcasted_iota(jnp.int32, (nrow, WINDOW), 1)
    wt = lax.broadcasted_iota(jnp.int32, (nrow, WINDOW), 0) % DEC_SEQ
    s_w = jnp.dot(q2b, wkt, preferred_element_type=F32) + bw_ref[...]
    s_w = jnp.where(wlane > wt, s_w, NEG_INF)
    wkn = _pad_rows(wnew_ref[0, :, 0:LANES]).astype(BF16)
    wvn = _pad_rows(wnew_ref[0, :, LANES:2 * LANES]).astype(BF16)
    s_wn = lax.dot_general(q2b, wkn, _NT, preferred_element_type=F32) + bn_ref[...]
    s_wn = jnp.where(lane <= t_row, s_wn, NEG_INF)
    o_win = _softmax_pv([(s_w, wvt, True), (s_wn, wvn, False)])

    g = jax.nn.sigmoid(gate_ref[0])
    o = g[:, 0:1] * o_cmp + g[:, 1:2] * o_slc + g[:, 2:3] * o_win
    o_ref[0] = jnp.concatenate([_pair_out(o, 2 * m_ * DEC_SEQ, (2 * m_ + 1) * DEC_SEQ)
                                for m_ in range(C_HEADS // 2)], axis=1)


def _nsa_sample(pt, q_s, c_new, w_new, gates, win_state, layer, cache, tables, pe2, w1bd, w2bd):
    nbt = q_s.shape[0]
    bc, bp, bn, bw, cov, expand = tables
    row = lambda r, w: pl.BlockSpec((1, r, w), lambda b, pt: (b, 0, 0))
    win_spec = pl.BlockSpec((1, CWIN_COLS, WINDOW), lambda b, pt: (layer * nbt + b, 0, 0))
    const = lambda a: pl.BlockSpec(a.shape, lambda b, pt: (0,) * a.ndim)
    gs = pltpu.PrefetchScalarGridSpec(
        num_scalar_prefetch=1, grid=(nbt,),
        in_specs=[row(DEC_SEQ, Q_COLS), row(DEC_SEQ, CKV_COLS), row(DEC_SEQ, CWIN_COLS),
                  row(C_HEADS * DEC_SEQ, LANES), win_spec,
                  const(bc), const(bp), const(bn), const(bw), const(cov), const(expand),
                  const(pe2), const(w1bd), const(w2bd)] + _page_specs(CKV_COLS),
        out_specs=row(DEC_SEQ, C_HEADS * HEAD_DIM),
        scratch_shapes=[pltpu.VMEM((2, PAST_LEN, LANES), F32)])
    return pl.pallas_call(
        _nsa_s_kernel, grid_spec=gs,
        out_shape=jax.ShapeDtypeStruct((nbt, DEC_SEQ, C_HEADS * HEAD_DIM), F32),
        compiler_params=_cparams(("arbitrary",)), name="nsa_sample",
    )(pt, q_s, c_new, w_new, gates, win_state, bc, bp, bn, bw, cov, expand, pe2, w1bd, w2bd,
      *([cache] * N_PAGES))


def _bucket_of(dist):
    return _BUCKETS[np.clip(dist, 0, MAX_DISTANCE)].astype(np.int32)


_SAMPLE_KEY_POS = np.concatenate([
    np.arange(PAST_LEN),
    PAST_LEN + np.arange(LANES),
    PAST_LEN - WINDOW + np.arange(WINDOW),
    np.arange(PAST_LEN // CMP_STRIDE) * CMP_STRIDE + CMP_BLOCK - 1
])


def _bias_bucket_tables(t):
    k = np.arange(TQ)[:, None]
    q = np.arange(TQ)[None, :]
    ncp = t // CMP_STRIDE
    rel = np.arange(2 * ncp)[:, None] - ncp
    band = _bucket_of(q - (rel * CMP_STRIDE + CMP_BLOCK - 1))
    samp = _bucket_of(PAST_LEN + np.arange(DEC_SEQ)[:, None] - _SAMPLE_KEY_POS[None, :])
    masked = np.full((TQ, TQ), _MASKED_CODE, np.int32)
    own = np.where(q >= k, _bucket_of(q - k), masked)
    prev = _bucket_of(TQ + q - k)
    far = _bucket_of(np.full((TQ, TQ), 2 * TQ))
    far_win = np.where(k > q, far, masked)
    tiles3 = np.concatenate([own, prev, far], axis=0)
    tiles5 = np.concatenate([far_win, prev, own] + [masked] * (WINDOW // TQ), axis=0)
    return [jnp.asarray(a) for a in (tiles3, tiles5, band, samp)]


_MASKED_CODE = N_BUCKETS


def _bias_kernel(tab_ref, *refs):
    h = pl.program_id(0)
    n = len(refs) // 2
    for idx_ref, o_ref in zip(refs[:n], refs[n:]):
        idx = idx_ref[...]
        acc = jnp.full(idx.shape, NEG_INF, F32)
        for bkt in range(N_BUCKETS):
            acc = jnp.where(idx == bkt, tab_ref[bkt, h], acc)
        o_ref[0] = acc


def _bias_tables(rel_bias, t):
    idx = _bias_bucket_tables(t)
    nh = rel_bias.shape[1]
    return pl.pallas_call(
        _bias_kernel,
        grid=(nh,),
        in_specs=[pl.BlockSpec(memory_space=pltpu.SMEM)]
        + [pl.BlockSpec(a.shape, lambda h: (0, 0)) for a in idx],
        out_specs=[pl.BlockSpec((1,) + a.shape, lambda h: (h, 0, 0)) for a in idx],
        out_shape=[jax.ShapeDtypeStruct((nh,) + a.shape, F32) for a in idx],
        compiler_params=_cparams(("arbitrary",)),
        name="bias_tables",
    )(rel_bias, *idx)


def _sample_masks():
    ncp = PAST_LEN // CMP_STRIDE
    nbs = PAST_LEN // SLC_BLOCK
    ci = np.arange(ncp)[:, None]
    bj = np.arange(LANES)[None, :]
    cov = ((ci * CMP_STRIDE < bj * SLC_BLOCK + SLC_BLOCK) & (ci * CMP_STRIDE + CMP_BLOCK > bj * SLC_BLOCK)
           & (ci < ncp - 1) & (bj < nbs))
    expand = np.arange(LANES)[:, None] == (np.arange(PAST_LEN)[None, :] // SLC_BLOCK)
    return jnp.asarray(cov, F32), jnp.asarray(expand, BF16)


def _prep_compress(cmp_pe, cmp_w1, cmp_w2):
    pe2 = jnp.concatenate([cmp_pe, cmp_pe], axis=-1)
    z1 = jnp.zeros_like(cmp_w1)
    w1bd = jnp.concatenate([jnp.concatenate([cmp_w1, z1], axis=-1),
                            jnp.concatenate([z1, cmp_w1], axis=-1)], axis=-2)
    z2 = jnp.zeros_like(cmp_w2)
    w2bd = jnp.concatenate([jnp.concatenate([cmp_w2, z2], axis=-1),
                            jnp.concatenate([z2, cmp_w2], axis=-1)], axis=-2)
    return pe2, w1bd.astype(BF16), w2bd.astype(BF16)


def _cover_t(nbs, ncp):
    cstart = np.arange(ncp)[None, :] * CMP_STRIDE
    bstart = np.arange(nbs)[:, None] * SLC_BLOCK
    cov = (cstart < bstart + SLC_BLOCK) & (cstart + CMP_BLOCK > bstart) & (np.arange(ncp)[None, :] < ncp - 1)
    return jnp.asarray(cov, F32)


_SPLIT_SIZES = (256, 256, 256, 384, 384, 384, 6, 384, 128, 128, 128, 128, 128, 128, 18)


def _prep_w_in(w_in):
    offs = np.cumsum(_SPLIT_SIZES)[:-1].tolist()
    aq, ak, av, bq, bk, bv, bf, cq, cck, ccv, csk, csv, cwk, cwv, cg = jnp.split(w_in, offs, axis=-1)
    lead = cq.shape[:-1]
    cq = jnp.concatenate([cq[..., h * HEAD_DIM:(h + 1) * HEAD_DIM] for h in C_SLOT_HEADS], axis=-1)
    scale = HEAD_DIM ** -0.5
    pad = jnp.zeros(lead + (SMALL_COLS - B_HEADS - C_HEADS * N_GATES,), w_in.dtype)
    w = jnp.concatenate([aq * scale, bq * scale, cq * scale, ak, av, bk, bv,
                         cck, ccv, csk, csv, cwk, cwv, bf, cg, pad], axis=-1)
    return w.astype(BF16)


def _prep_small_bias(b_f, b_gate):
    pad = jnp.zeros(b_f.shape[:-1] + (SMALL_COLS - B_HEADS - C_HEADS * N_GATES,), F32)
    return jnp.concatenate([b_f, b_gate, pad], axis=-1)[:, None, :]


def _prep_w_out(w_out):
    wa = w_out[:, :A_HEADS * HEAD_DIM]
    wb = w_out[:, A_HEADS * HEAD_DIM:(A_HEADS + B_HEADS) * HEAD_DIM]
    wc = w_out[:, (A_HEADS + B_HEADS) * HEAD_DIM:]
    wc = jnp.concatenate([wc[:, h * HEAD_DIM:(h + 1) * HEAD_DIM] for h in C_SLOT_HEADS], axis=1)
    return wa.astype(BF16), wb.astype(BF16), wc.astype(BF16)


def kernel(x_prompt, x_sample, cache_a_kv, cache_b_kv, cache_b_logf, cache_c_kv, state_c_win, state_ffn_conv,
           page_table, rel_bias, ln1_g, w_in, b_f, b_gate, cmp_pe, cmp_w1, cmp_w2, w_out, ln2_g, w_gate, w_up,
           conv_w, conv_b, w_down, final_g):
    depth = w_in.shape[0]
    b, t, _ = x_prompt.shape
    nbt, ts, _ = x_sample.shape
    n_pool = cache_a_kv.shape[1]
    assert ts == DEC_SEQ and page_table.shape[1] == N_PAGES and cache_a_kv.shape[2] == PAGE_SIZE
    assert t % TQ == 0 and state_c_win.shape[2] == WINDOW and t >= WINDOW
    ffn_tm = 1024

    w_in_p = _prep_w_in(w_in)
    small_bias = _prep_small_bias(b_f, b_gate)
    pe2, w1bd, w2bd = _prep_compress(cmp_pe, cmp_w1, cmp_w2)
    woa, wob, woc = _prep_w_out(w_out)
    wg, wu, wd = w_gate.astype(BF16), w_up.astype(BF16), w_down.astype(BF16)

    tiles3, tiles5, band, samp = _bias_tables(rel_bias, t)
    c_slots = lambda x: jnp.stack([x[A_HEADS + h] for h in C_SLOT_HEADS], axis=0)
    tiles3_a, tiles3_c, tiles5_c, band_c = tiles3[:A_HEADS], c_slots(tiles3), c_slots(tiles5), c_slots(band)
    cover_t = _cover_t(t // SLC_BLOCK, t // CMP_STRIDE)
    o1, o2, o3 = PAST_LEN, PAST_LEN + LANES, PAST_LEN + LANES + WINDOW
    samp_a = samp[:A_HEADS].reshape(A_HEADS // 2, 2 * DEC_SEQ, -1)
    moba_tabs = (samp_a[..., :o1], samp_a[..., o1:o2])
    samp_c = c_slots(samp).reshape(C_HEADS * DEC_SEQ, -1)
    nsa_tabs = (samp_c[:, o3:], samp_c[:, :o1], samp_c[:, o1:o2], samp_c[:, o2:o3]) + _sample_masks()

    fm = lambda c: jnp.transpose(c, (0, 1, 3, 4, 5, 2))
    cache_a = fm(cache_a_kv).reshape(depth * n_pool, AKV_COLS, PAGE_SIZE)
    cache_b = fm(cache_b_kv).reshape(depth * n_pool, BKV_COLS, PAGE_SIZE)
    cache_c = fm(cache_c_kv).reshape(depth * n_pool, CKV_COLS, PAGE_SIZE)
    cache_lf = jnp.transpose(cache_b_logf, (0, 3, 1, 2)).reshape(depth * B_HEADS, n_pool, PAGE_SIZE)
    win_all = fm(state_c_win).reshape(depth * nbt, CWIN_COLS, WINDOW)
    pt_flat = page_table.reshape(-1).astype(jnp.int32)

    xp = x_prompt.reshape(b * t, D_MODEL)
    xs = x_sample.reshape(nbt * ts, D_MODEL)
    outs_p = [[] for _ in range(6)]
    outs_s = [[] for _ in range(6)]
    for l in range(depth):
        final = l == depth - 1
        q, akv, bkv, ckv, cwin, small = [o.reshape(b, t, -1) for o in
                                         _in_proj(xp, ln1_g[l], w_in_p[l], small_bias[l])]
        ya = _moba_prompt(q, akv, tiles3_a)
        yb = _fox_prompt(q, bkv, small)
        kc, vc = _compress_prompt(ckv, pe2[l], w1bd[l], w2bd[l])
        yc = _nsa_prompt(q, ckv, cwin, jnp.swapaxes(small, 1, 2), tiles3_c, tiles5_c, band_c, cover_t, kc, vc)
        xp, gk = _ffn(xp, ya.reshape(b * t, -1), yb.reshape(b * t, -1), yc.reshape(b * t, -1),
                      woa[l], wob[l], woc[l], ln2_g[l], final_g, wg[l], wu[l], wd[l], conv_w[l], conv_b[l],
                      sample=False, blocks_per_seq=t // ffn_tm, final=final, tm=ffn_tm)
        outs_p[0].append(akv.reshape(b, t, 2, A_HEADS, HEAD_DIM))
        outs_p[1].append(bkv.reshape(b, t, 2, B_HEADS, HEAD_DIM))
        outs_p[2].append(small[..., :B_HEADS])
        outs_p[3].append(ckv.reshape(b, t, 4, C_KV_HEADS, HEAD_DIM))
        outs_p[4].append(cwin[:, t - WINDOW:].reshape(b, WINDOW, 2, C_KV_HEADS, HEAD_DIM))
        outs_p[5].append(gk.reshape(b, t // ffn_tm, SUBLANES, D_FF)[:, -1, SUBLANES - (CONV_W - 1):])
        q, akv, bkv, ckv, cwin, small = [o.reshape(nbt, ts, -1) for o in
                                         _in_proj(xs, ln1_g[l], w_in_p[l], small_bias[l])]
        pt = pt_flat + l * n_pool
        ya = _moba_sample(pt, q, akv, cache_a, *moba_tabs)
        yb = _fox_sample(pt, pt_flat, q, bkv, small, cache_b, cache_lf, l)
        gl = small[..., B_HEADS:B_HEADS + C_HEADS * N_GATES].reshape(nbt, ts, C_HEADS, N_GATES)
        gl = jnp.concatenate([gl[:, :, h] for h in C_SLOT_HEADS], axis=1)
        gl = jnp.pad(gl, ((0, 0), (0, 0), (0, LANES - N_GATES)))
        yc = _nsa_sample(pt, q, ckv, cwin, gl, win_all, l, cache_c, nsa_tabs, pe2[l], w1bd[l], w2bd[l])
        prev = state_ffn_conv[l]
        zero = jnp.zeros((nbt, ts - 2, D_FF), F32)
        p1 = jnp.concatenate([prev[:, 1:2], zero, zero[:, :1]], axis=1).reshape(nbt * ts, D_FF)
        p2 = jnp.concatenate([prev, zero], axis=1).reshape(nbt * ts, D_FF)
        xs, g_s = _ffn(xs, ya.reshape(nbt * ts, -1), yb.reshape(nbt * ts, -1), yc.reshape(nbt * ts, -1),
                       woa[l], wob[l], woc[l], ln2_g[l], final_g, wg[l], wu[l], wd[l], conv_w[l], conv_b[l],
                       sample=True, final=final, inj=(p1, p2), tm=ffn_tm)
        outs_s[0].append(akv.reshape(nbt, ts, 2, A_HEADS, HEAD_DIM))
        outs_s[1].append(bkv.reshape(nbt, ts, 2, B_HEADS, HEAD_DIM))
        outs_s[2].append(small[..., :B_HEADS])
        outs_s[3].append(ckv.reshape(nbt, ts, 4, C_KV_HEADS, HEAD_DIM))
        outs_s[4].append(jnp.concatenate([state_c_win[l][:, ts:],
                                          cwin.reshape(nbt, ts, 2, C_KV_HEADS, HEAD_DIM)], axis=1))
        outs_s[5].append(g_s.reshape(nbt, ts, D_FF)[:, ts - (CONV_W - 1):])
    st = lambda rows: jnp.stack(rows, axis=0)
    return (xp.reshape(b, t, D_MODEL), xs.reshape(nbt, ts, D_MODEL),
            *[st(r) for r in outs_p], *[st(r) for r in outs_s])
```

```python
import functools
import math

import numpy as np
import jax
import jax.numpy as jnp
from jax import lax
from jax.experimental import pallas as pl
from jax.experimental.pallas import tpu as pltpu

F32 = jnp.float32
BF16 = jnp.bfloat16
HIGHEST = lax.Precision.HIGHEST

D_MODEL = 1024
HEAD_DIM = 64
A_HEADS = 4
B_HEADS = 6
C_HEADS = 6
C_KV_HEADS = 2
C_GROUP = 3
MOBA_BLOCK = 256
MOBA_TOPK = 3
CMP_BLOCK = 32
CMP_STRIDE = 16
CMP_HIDDEN = 128
SLC_BLOCK = 64
SLC_TOPK = 8
WINDOW = 512
N_GATES = 3
N_BUCKETS = 32
MAX_DISTANCE = 128
D_FF = 2816
CONV_W = 3
RMS_EPS = 1e-6
NEG_INF = -1e30
FORCE_SCORE = 1e6
PAGE_SIZE = 128

LANES = 128
SUBLANES = 8
TQ = 256
VMEM_LIMIT = 56 * 1024 * 1024

C_SLOT_HEADS = (0, 3, 1, 4, 2, 5)

Q_COLS = (A_HEADS + B_HEADS + C_HEADS) * HEAD_DIM
AKV_COLS = 2 * A_HEADS * HEAD_DIM
BKV_COLS = 2 * B_HEADS * HEAD_DIM
CKV_COLS = 4 * C_KV_HEADS * HEAD_DIM
CWIN_COLS = 2 * C_KV_HEADS * HEAD_DIM
SMALL_COLS = LANES
IN_COLS = Q_COLS + AKV_COLS + BKV_COLS + CKV_COLS + CWIN_COLS + SMALL_COLS

_NT = (((1,), (1,)), ((), ()))


def _bucket_table():
    n = np.arange(MAX_DISTANCE + 1)
    exact = N_BUCKETS // 2
    nf = np.maximum(n, 1).astype(np.float32)
    large = exact + (np.log(nf / exact) / math.log(MAX_DISTANCE / exact) * (N_BUCKETS - exact)).astype(np.int32)
    large = np.minimum(large, N_BUCKETS - 1)
    return np.where(n < exact, n, large).astype(np.int32)


_BUCKETS = _bucket_table()


def _cparams(sem, vmem=VMEM_LIMIT):
    return pltpu.CompilerParams(dimension_semantics=sem, vmem_limit_bytes=vmem)


def _gelu(x):
    return 0.5 * x * (1.0 + jnp.tanh(math.sqrt(2.0 / math.pi) * (x + 0.044715 * (x * x * x))))


def _log_sigmoid(x):
    return -(jnp.maximum(-x, 0.0) + jnp.log1p(jnp.exp(-jnp.abs(x))))


def _in_proj_kernel(x_ref, g_ref, w_ref, sb_ref, q_ref, akv_ref, bkv_ref, ckv_ref, cwin_ref, small_ref):
    x = x_ref[...]
    ms = jnp.mean(x * x, axis=-1, keepdims=True)
    h = (x * lax.rsqrt(ms + RMS_EPS)) * g_ref[...]
    hb = h.astype(BF16)
    off = 0
    for ref in (q_ref, akv_ref, bkv_ref, ckv_ref, cwin_ref):
        n = ref.shape[-1]
        ref[...] = jnp.dot(hb, w_ref[:, off:off + n], preferred_element_type=F32)
        off += n
    raw = jnp.dot(hb, w_ref[:, off:off + SMALL_COLS], preferred_element_type=F32) + sb_ref[...]
    lane = lax.broadcasted_iota(jnp.int32, raw.shape, 1)
    small_ref[...] = jnp.where(lane < B_HEADS, _log_sigmoid(raw), raw)


def _in_proj(x2d, g, w, small_bias, tm=256):
    n = x2d.shape[0]
    tm = min(tm, n)
    widths = (Q_COLS, AKV_COLS, BKV_COLS, CKV_COLS, CWIN_COLS, SMALL_COLS)
    return pl.pallas_call(
        _in_proj_kernel,
        grid=(n // tm,),
        in_specs=[pl.BlockSpec((tm, D_MODEL), lambda i: (i, 0)),
                  pl.BlockSpec((1, D_MODEL), lambda i: (0, 0)),
                  pl.BlockSpec((D_MODEL, IN_COLS), lambda i: (0, 0)),
                  pl.BlockSpec((1, SMALL_COLS), lambda i: (0, 0))],
        out_specs=[pl.BlockSpec((tm, c), lambda i: (i, 0)) for c in widths],
        out_shape=[jax.ShapeDtypeStruct((n, c), F32) for c in widths],
        compiler_params=_cparams(("parallel",)),
        name="in_proj",
    )(x2d, g.reshape(1, D_MODEL), w, small_bias)


def _flash_init(s, vt):
    m = jnp.max(s, axis=0, keepdims=True)
    p = jnp.exp(s - m)
    l = jnp.sum(p, axis=0, keepdims=True)
    acc = jnp.dot(vt, p.astype(BF16), preferred_element_type=F32)
    return m, l, acc


def _flash_step(s, vt, m, l, acc):
    m_new = jnp.maximum(m, jnp.max(s, axis=0, keepdims=True))
    a = jnp.exp(m - m_new)
    p = jnp.exp(s - m_new)
    l = a * l + jnp.sum(p, axis=0, keepdims=True)
    acc = a * acc + jnp.dot(vt, p.astype(BF16), preferred_element_type=F32)
    return m_new, l, acc


def _flash_pipeline(n_tiles, score_fn, vt_fn, sbuf, heads=2, first=None):
    for h in range(heads):
        sbuf[h] = score_fn(h, 0) if first is None else first(h)
    pad = 2 * SUBLANES
    ones = jnp.ones((pad, TQ), BF16)
    init = []
    for h in range(heads):
        init += [jnp.full((1, TQ), NEG_INF, F32), jnp.zeros((HEAD_DIM + pad, TQ), F32)]

    def body(i, st):
        cur = [sbuf[h] for h in range(heads)]
        nxt = [score_fn(h, jnp.minimum(i + 1, n_tiles - 1)) for h in range(heads)]
        out = []
        for h in range(heads):
            m, acc = st[2 * h:2 * h + 2]
            m_new = jnp.maximum(m, jnp.max(cur[h], axis=0, keepdims=True))
            p = jnp.exp(cur[h] - m_new).astype(BF16)
            vt1 = jnp.concatenate([vt_fn(h, i), ones], axis=0)
            acc = jnp.exp(m - m_new) * acc + jnp.dot(vt1, p, preferred_element_type=F32)
            out.extend((m_new, acc))
        for h in range(heads):
            sbuf[h] = nxt[h]
        return tuple(out)

    st = lax.fori_loop(0, n_tiles, body, tuple(init))
    out = []
    for h in range(heads):
        acc = st[2 * h + 1]
        out.extend((st[2 * h], acc[HEAD_DIM:HEAD_DIM + 1, :], acc[0:HEAD_DIM, :]))
    return out


def _top_rank(vals, row):
    rank = jnp.zeros(vals.shape, F32)
    for j in range(vals.shape[0]):
        vj = vals[j:j + 1, :]
        beats = (vj > vals) | ((vj == vals) & (j < row))
        rank = rank + beats.astype(F32)
    return rank


def _head_mask(q, h):
    lane = lax.broadcasted_iota(jnp.int32, (1, LANES), 1)
    return jnp.where((lane // HEAD_DIM) == h, q, 0.0)


def _moba_p_kernel(q0_ref, q1_ref, k_ref, v_ref, b3_ref, o_ref, kb, vtb, kmean, sel, sbuf):
    qi = pl.program_id(1)
    nb = kmean.shape[0]

    @pl.when(qi == 0)
    def _prep():
        k = k_ref[0]
        kb[...] = k.astype(BF16)
        vtb[...] = v_ref[0].T.astype(BF16)
        for j in range(nb):
            kmean[j:j + 1, :] = jnp.mean(k[j * TQ:(j + 1) * TQ, :], axis=0, keepdims=True)

    rowj = lax.broadcasted_iota(jnp.int32, (nb, TQ), 0)
    qbs = []
    for h in range(A_HEADS):
        pair = slice((h // 2) * LANES, (h // 2 + 1) * LANES)
        qh = _head_mask((q0_ref, q1_ref)[h // 2][0], h % 2)
        qbs.append(qh.astype(BF16))
        g = lax.dot_general(kmean[:, pair], qh, _NT, precision=HIGHEST, preferred_element_type=F32)
        gm = jnp.where(rowj < qi, g, NEG_INF)
        chosen = ((_top_rank(gm, rowj) < MOBA_TOPK) & (rowj < qi)) | (rowj == qi)
        sel[h] = jnp.where(chosen, 0.0, NEG_INF)

    def score_fn(h, i):
        j = qi - i
        kblk = kb[pl.ds(pl.multiple_of(j * TQ, TQ), TQ), (h // 2) * LANES:(h // 2 + 1) * LANES]
        bias = b3_ref[h, pl.ds(pl.multiple_of(jnp.minimum(i, 2) * TQ, TQ), TQ), :]
        return lax.dot_general(kblk, qbs[h], _NT, preferred_element_type=F32) + bias + sel[h, pl.ds(j, 1), :]

    def vt_fn(h, i):
        return vtb[h * HEAD_DIM:(h + 1) * HEAD_DIM, pl.ds(pl.multiple_of((qi - i) * TQ, TQ), TQ)]

    st = _flash_pipeline(qi + 1, score_fn, vt_fn, sbuf, heads=A_HEADS)
    o = jnp.concatenate([st[3 * h + 2] / st[3 * h + 1] for h in range(A_HEADS)], axis=0)
    o_ref[0] = o.T


def _moba_prompt(q_all, a_kv, tiles3):
    b, t, _ = q_all.shape
    width = A_HEADS * HEAD_DIM
    nb = t // TQ
    return pl.pallas_call(
        _moba_p_kernel,
        grid=(b, nb),
        in_specs=[pl.BlockSpec((1, TQ, LANES), lambda bi, qi: (bi, qi, 0)),
                  pl.BlockSpec((1, TQ, LANES), lambda bi, qi: (bi, qi, 1)),
                  pl.BlockSpec((1, t, width), lambda bi, qi: (bi, 0, 0)),
                  pl.BlockSpec((1, t, width), lambda bi, qi: (bi, 0, 1)),
                  pl.BlockSpec((A_HEADS, 3 * TQ, TQ), lambda bi, qi: (0, 0, 0))],
        out_specs=pl.BlockSpec((1, TQ, width), lambda bi, qi: (bi, qi, 0)),
        out_shape=jax.ShapeDtypeStruct((b, t, width), F32),
        scratch_shapes=[pltpu.VMEM((t, width), BF16), pltpu.VMEM((width, t), BF16),
                        pltpu.VMEM((nb, width), F32), pltpu.VMEM((A_HEADS, nb, TQ), F32),
                        pltpu.VMEM((A_HEADS, TQ, TQ), F32)],
        compiler_params=_cparams(("parallel", "arbitrary")),
        name="moba_prompt",
    )(q_all, q_all, a_kv, a_kv, tiles3)


def _fox_p_kernel(q0_ref, q1_ref, q2_ref, k_ref, v_ref, lf_ref, o_ref, kb, vtb, ckb, sbuf):
    qi = pl.program_id(1)
    t = kb.shape[0]
    nb = t // TQ

    @pl.when(qi == 0)
    def _prep():
        kb[...] = k_ref[0].astype(BF16)
        vtb[...] = v_ref[0].T.astype(BF16)
        r = lax.broadcasted_iota(jnp.int32, (TQ, TQ), 0)
        c = lax.broadcasted_iota(jnp.int32, (TQ, TQ), 1)
        tri = (c <= r).astype(BF16)
        sr = lax.broadcasted_iota(jnp.int32, (LANES, LANES), 0)
        parts = _split3(lf_ref[0])
        for h in range(B_HEADS):
            pick = (sr == h).astype(BF16)
            sel3 = jnp.concatenate([jnp.dot(x, pick, preferred_element_type=F32).astype(BF16) for x in parts],
                                   axis=1)
            carry = jnp.zeros((1, LANES), F32)
            for j in range(nb):
                c3 = jnp.dot(tri, sel3[j * TQ:(j + 1) * TQ, :], preferred_element_type=F32)
                blk = c3[:, 0:LANES] + c3[:, LANES:2 * LANES] + c3[:, 2 * LANES:3 * LANES] + carry
                ckb[h, j * TQ:(j + 1) * TQ, :] = blk
                carry = blk[TQ - 1:TQ, :]

    q_refs = (q0_ref, q1_ref, q2_ref)
    qbs = [_head_mask(q_refs[h // 2][0], h % 2).astype(BF16) for h in range(B_HEADS)]
    kk = lax.broadcasted_iota(jnp.int32, (TQ, LANES), 0)
    for h in range(B_HEADS):
        own = ckb[h, pl.ds(pl.multiple_of(qi * TQ, TQ), TQ), :]
        for half in range(TQ // LANES):
            qq = lax.broadcasted_iota(jnp.int32, (TQ, LANES), 1) + half * LANES
            ckb[h, t + half * TQ:t + (half + 1) * TQ, :] = jnp.where(kk <= qq, own, -NEG_INF)

    def k_tile(h, j):
        return kb[pl.ds(pl.multiple_of(j * TQ, TQ), TQ), (h // 2) * LANES:(h // 2 + 1) * LANES]

    def score_fn(h, i):
        j = qi - i
        s = lax.dot_general(k_tile(h, j), qbs[h], _NT, preferred_element_type=F32)
        c = ckb[h, pl.ds(pl.multiple_of(j * TQ, TQ), TQ), :]
        return s - jnp.concatenate([c, c], axis=1)

    def score_own(h):
        s = lax.dot_general(k_tile(h, qi), qbs[h], _NT, preferred_element_type=F32)
        return s - jnp.concatenate([ckb[h, t:t + TQ, :], ckb[h, t + TQ:t + 2 * TQ, :]], axis=1)

    def vt_fn(h, i):
        return vtb[h * HEAD_DIM:(h + 1) * HEAD_DIM, pl.ds(pl.multiple_of((qi - i) * TQ, TQ), TQ)]

    st = _flash_pipeline(qi + 1, score_fn, vt_fn, sbuf, heads=B_HEADS, first=score_own)
    o = jnp.concatenate([st[3 * h + 2] / st[3 * h + 1] for h in range(B_HEADS)], axis=0)
    o_ref[0] = o.T


def _fox_prompt(q_all, b_kv, small):
    b, t, _ = q_all.shape
    width = B_HEADS * HEAD_DIM
    qoff = A_HEADS // 2
    nb = t // TQ
    qspec = lambda m: pl.BlockSpec((1, TQ, LANES), lambda bi, qi: (bi, qi, qoff + m))
    return pl.pallas_call(
        _fox_p_kernel,
        grid=(b, nb),
        in_specs=[qspec(0), qspec(1), qspec(2),
                  pl.BlockSpec((1, t, width), lambda bi, qi: (bi, 0, 0)),
                  pl.BlockSpec((1, t, width), lambda bi, qi: (bi, 0, 1)),
                  pl.BlockSpec((1, t, LANES), lambda bi, qi: (bi, 0, 0))],
        out_specs=pl.BlockSpec((1, TQ, width), lambda bi, qi: (bi, qi, 0)),
        out_shape=jax.ShapeDtypeStruct((b, t, width), F32),
        scratch_shapes=[pltpu.VMEM((t, width), BF16), pltpu.VMEM((width, t), BF16),
                        pltpu.VMEM((B_HEADS, t + 2 * TQ, LANES), F32), pltpu.VMEM((B_HEADS, TQ, TQ), F32)],
        compiler_params=_cparams(("parallel", "arbitrary")),
        name="fox_prompt",
    )(q_all, q_all, q_all, b_kv, b_kv, small)


def _compress_pair(rows_of, ncp, pe_ref, w1_ref, w2_ref, kind):
    half = CMP_BLOCK // 2
    acc_a = jnp.zeros((ncp, 2 * CMP_HIDDEN), F32)
    acc_b = jnp.zeros((ncp, 2 * CMP_HIDDEN), F32)
    for l in range(half):
        r = rows_of(l)
        xa = (r + pe_ref[kind, l:l + 1, :]).astype(BF16)
        xb = (r + pe_ref[kind, half + l:half + l + 1, :]).astype(BF16)
        acc_a = acc_a + jnp.dot(xa, w1_ref[kind, l], preferred_element_type=F32)
        acc_b = acc_b + jnp.dot(xb, w1_ref[kind, half + l], preferred_element_type=F32)
    pre = acc_a + pltpu.roll(acc_b, ncp - 1, 0)
    hdn = _gelu(pre).astype(BF16)
    return jnp.dot(hdn, w2_ref[kind], preferred_element_type=F32)


def _compress_p_kernel(ck_ref, cv_ref, pe_ref, w1_ref, w2_ref, kc_ref, vc_ref):
    ncp = kc_ref.shape[1]

    def rows_of(ref):
        return lambda l: ref[0, pl.ds(l, ncp, stride=CMP_STRIDE), :]

    kc_ref[0] = _compress_pair(rows_of(ck_ref), ncp, pe_ref, w1_ref, w2_ref, 0).astype(BF16)
    vc_ref[0] = _compress_pair(rows_of(cv_ref), ncp, pe_ref, w1_ref, w2_ref, 1)


def _compress_prompt(c_kv, pe2, w1bd, w2bd):
    b, t, _ = c_kv.shape
    ncp = t // CMP_STRIDE
    const = lambda *shape: pl.BlockSpec(shape, lambda bi: (0,) * len(shape))
    return pl.pallas_call(
        _compress_p_kernel,
        grid=(b,),
        in_specs=[pl.BlockSpec((1, t, LANES), lambda bi: (bi, 0, 0)),
                  pl.BlockSpec((1, t, LANES), lambda bi: (bi, 0, 1)),
                  const(2, CMP_BLOCK, LANES), const(2, CMP_BLOCK, LANES, 2 * CMP_HIDDEN),
                  const(2, 2 * CMP_HIDDEN, LANES)],
        out_specs=[pl.BlockSpec((1, ncp, LANES), lambda bi: (bi, 0, 0)),
                   pl.BlockSpec((1, ncp, LANES), lambda bi: (bi, 0, 0))],
        out_shape=[jax.ShapeDtypeStruct((b, ncp, LANES), BF16), jax.ShapeDtypeStruct((b, ncp, LANES), F32)],
        compiler_params=_cparams(("parallel",)),
        name="compress_prompt",
    )(c_kv, c_kv, pe2, w1bd, w2bd)


def _nsa_p_kernel(q0_ref, q1_ref, q2_ref, ckv_ref, cwin_ref, gt_ref, t3_ref, w5_ref, bc_ref, cov_ref,
                  kcb, vc_ref, o_ref, vctb, skb, svtb, wkb, wvtb, selsc, sbuf):
    qi = pl.program_id(1)
    t = skb.shape[0]
    ncp = kcb.shape[1]
    nbs = selsc.shape[1]
    per_tile = TQ // SLC_BLOCK

    @pl.when(qi == 0)
    def _prep():
        vctb[...] = vc_ref[0].T.astype(BF16)
        skb[...] = ckv_ref[0, :, 0:LANES].astype(BF16)
        svtb[...] = ckv_ref[0, :, LANES:2 * LANES].T.astype(BF16)
        wkb[...] = cwin_ref[0, :, 0:LANES].astype(BF16)
        wvtb[...] = cwin_ref[0, :, LANES:2 * LANES].T.astype(BF16)

    q_refs = (q0_ref, q1_ref, q2_ref)
    qpos = qi * TQ + lax.broadcasted_iota(jnp.int32, (1, TQ), 1)
    ci = lax.broadcasted_iota(jnp.int32, (ncp, TQ), 0)
    cok = (qpos >= ci * CMP_STRIDE + (CMP_BLOCK - 1)) & (ci < ncp - 1)

    qbs, ocmp = [], []
    psum = [jnp.zeros((ncp, TQ), F32), jnp.zeros((ncp, TQ), F32)]
    for s in range(C_HEADS):
        kvh = s % 2
        qb = _head_mask(q_refs[s // 2][0], kvh).astype(BF16)
        qbs.append(qb)
        band0 = pl.multiple_of(ncp - qi * (TQ // CMP_STRIDE), TQ // CMP_STRIDE)
        sc = lax.dot_general(kcb[0], qb, _NT, preferred_element_type=F32) + bc_ref[s, pl.ds(band0, ncp), :]
        sc = jnp.where(cok, sc, NEG_INF)
        m = jnp.max(sc, axis=0, keepdims=True)
        e = jnp.where(cok, jnp.exp(sc - m), 0.0)
        den = jnp.sum(e, axis=0, keepdims=True)
        pc = e * jnp.where(den > 0.0, 1.0 / den, 0.0)
        psum[kvh] = psum[kvh] + pc
        ocmp.append(jnp.dot(vctb[kvh * HEAD_DIM:(kvh + 1) * HEAD_DIM, :], pc.astype(BF16),
                            preferred_element_type=F32))

    jj = lax.broadcasted_iota(jnp.int32, (nbs, TQ), 0)
    own = qpos // SLC_BLOCK
    forced = (jj == 0) | (jj == own) | (jj == own - 1)
    for kvh in range(2):
        imp = jnp.dot(cov_ref[...], psum[kvh], precision=HIGHEST, preferred_element_type=F32)
        v = jnp.where(jj <= own, jnp.where(forced, FORCE_SCORE, imp), NEG_INF)
        chosen = (_top_rank(v, jj) < SLC_TOPK) & (jj <= own)
        selsc[kvh] = jnp.where(chosen, 0.0, NEG_INF)

    def sel_rows(kvh, j):
        rows = [jnp.broadcast_to(selsc[kvh, pl.ds(j * per_tile + r, 1), :], (SLC_BLOCK, TQ))
                for r in range(per_tile)]
        return jnp.concatenate(rows, axis=0)

    wn = WINDOW + TQ
    wstart = pl.multiple_of(jnp.maximum(qi - WINDOW // TQ, 0) * TQ, TQ)
    woff = pl.multiple_of((WINDOW // TQ - jnp.minimum(qi, WINDOW // TQ)) * TQ, TQ)

    def score_fn(s, i):
        j = qi - i
        kblk = skb[pl.ds(pl.multiple_of(j * TQ, TQ), TQ), :]
        bias = t3_ref[s, pl.ds(pl.multiple_of(jnp.minimum(i, 2) * TQ, TQ), TQ), :]
        return lax.dot_general(kblk, qbs[s], _NT, preferred_element_type=F32) + bias + sel_rows(s % 2, j)

    def vt_fn(s, i):
        kvh = s % 2
        return svtb[kvh * HEAD_DIM:(kvh + 1) * HEAD_DIM, pl.ds(pl.multiple_of((qi - i) * TQ, TQ), TQ)]

    st_slc = _flash_pipeline(qi + 1, score_fn, vt_fn, sbuf, heads=C_HEADS)

    for pair in range(C_HEADS // 2):
        outs = []
        for s in (2 * pair, 2 * pair + 1):
            n = s
            kvh = s % 2
            sc = lax.dot_general(wkb[pl.ds(wstart, wn), :], qbs[s], _NT, preferred_element_type=F32)
            sc = sc + w5_ref[s, pl.ds(woff, wn), :]
            m, l, acc = _flash_init(sc, wvtb[kvh * HEAD_DIM:(kvh + 1) * HEAD_DIM, pl.ds(wstart, wn)])
            o_win = acc / l
            o_slc = st_slc[3 * n + 2] / st_slc[3 * n + 1]
            grow = B_HEADS + C_SLOT_HEADS[s] * N_GATES
            g = [jax.nn.sigmoid(gt_ref[0, grow + n_:grow + n_ + 1, :]) for n_ in range(N_GATES)]
            outs.append(g[0] * ocmp[s] + g[1] * o_slc + g[2] * o_win)
        o_ref[0, :, pair * LANES:(pair + 1) * LANES] = jnp.concatenate(outs, axis=0).T


def _nsa_prompt(q_all, c_kv, c_win, small_t, tiles3, tiles5, bc, cover_t, kc, vct):
    b, t, _ = q_all.shape
    assert t >= WINDOW + TQ
    nq = t // TQ
    ncp = t // CMP_STRIDE
    nbs = t // SLC_BLOCK
    qoff = (A_HEADS + B_HEADS) // 2
    const = lambda *shape: pl.BlockSpec(shape, lambda bi, qi: (0,) * len(shape))
    qspec = lambda m: pl.BlockSpec((1, TQ, LANES), lambda bi, qi: (bi, qi, qoff + m))
    return pl.pallas_call(
        _nsa_p_kernel,
        grid=(b, nq),
        in_specs=[qspec(0), qspec(1), qspec(2),
                  pl.BlockSpec((1, t, 2 * LANES), lambda bi, qi: (bi, 0, 1)),
                  pl.BlockSpec((1, t, CWIN_COLS), lambda bi, qi: (bi, 0, 0)),
                  pl.BlockSpec((1, SMALL_COLS, TQ), lambda bi, qi: (bi, 0, qi)),
                  const(C_HEADS, 3 * TQ, TQ),
                  const(C_HEADS, WINDOW + 3 * TQ, TQ),
                  const(C_HEADS, 2 * ncp, TQ),
                  const(nbs, ncp),
                  pl.BlockSpec((1, ncp, LANES), lambda bi, qi: (bi, 0, 0)),
                  pl.BlockSpec((1, ncp, LANES), lambda bi, qi: (bi, 0, 0))],
        out_specs=pl.BlockSpec((1, TQ, C_HEADS * HEAD_DIM), lambda bi, qi: (bi, qi, 0)),
        out_shape=jax.ShapeDtypeStruct((b, t, C_HEADS * HEAD_DIM), F32),
        scratch_shapes=[pltpu.VMEM((LANES, ncp), BF16),
                        pltpu.VMEM((t, LANES), BF16), pltpu.VMEM((LANES, t), BF16),
                        pltpu.VMEM((t, LANES), BF16), pltpu.VMEM((LANES, t), BF16),
                        pltpu.VMEM((2, nbs, TQ), F32), pltpu.VMEM((C_HEADS, TQ, TQ), F32)],
        compiler_params=_cparams(("parallel", "arbitrary")),
        name="nsa_prompt",
    )(q_all, q_all, q_all, c_kv, c_win, small_t, tiles3, tiles5, bc, cover_t, kc, vct)


FF_CHUNK = 256


def _rms(x, g):
    ms = jnp.mean(x * x, axis=-1, keepdims=True)
    return (x * lax.rsqrt(ms + RMS_EPS)) * g


def _ffn_kernel(sample, blocks_per_seq, final, x_ref, ya_ref, yb_ref, yc_ref, woa_ref, wob_ref, woc_ref,
                g2_ref, gf_ref, wg_ref, wu_ref, wd_ref, cw_ref, cb_ref, *rest):
    if sample:
        p1_ref, p2_ref, o_ref, gk_ref, x1_s, h2_s, act_s = rest
    else:
        o_ref, gk_ref, x1_s, h2_s, act_s, halo_s = rest
    i = pl.program_id(0)
    c = pl.program_id(1)
    last = pl.num_programs(1) - 1
    tm = x_ref.shape[0]

    def activate(cc, g, u, gs1, gs2):
        col = pl.ds(pl.multiple_of(cc * FF_CHUNK, FF_CHUNK), FF_CHUNK)
        conv = cb_ref[:, col] + cw_ref[0:1, col] * gs2
        conv = conv + cw_ref[1:2, col] * gs1
        conv = conv + cw_ref[2:3, col] * g
        act_s[:, col] = (_gelu(conv) * u).astype(BF16)

    @pl.when(c == 0)
    def _attn_out():
        y = (jnp.dot(ya_ref[...].astype(BF16), woa_ref[...], preferred_element_type=F32)
             + jnp.dot(yb_ref[...].astype(BF16), wob_ref[...], preferred_element_type=F32)
             + jnp.dot(yc_ref[...].astype(BF16), woc_ref[...], preferred_element_type=F32))
        x1 = x_ref[...] + y
        x1_s[...] = x1
        h2_s[...] = _rms(x1, g2_ref[...]).astype(BF16)

    hb = h2_s[...]
    g_new = jnp.dot(hb, wg_ref[...], preferred_element_type=F32)
    u_new = jnp.dot(hb, wu_ref[...], preferred_element_type=F32)
    row = lax.broadcasted_iota(jnp.int32, g_new.shape, 0)
    if sample:
        t = row % SUBLANES
        gs1 = jnp.where(t >= 1, pltpu.roll(g_new, 1, 0), p1_ref[...])
        gs2 = jnp.where(t >= 2, pltpu.roll(g_new, 2, 0), p2_ref[...])
        gk_ref[...] = g_new
        activate(c, g_new, u_new, gs1, gs2)
    else:
        halo = jnp.where(i % blocks_per_seq == 0, 0.0, halo_s[c])
        gs1 = jnp.where(row >= 1, pltpu.roll(g_new, 1, 0), halo[SUBLANES - 1:SUBLANES, :])
        gs2 = jnp.where(row >= 2, pltpu.roll(g_new, 2, 0),
                        jnp.where(row == 0, halo[SUBLANES - 2:SUBLANES - 1, :], halo[SUBLANES - 1:SUBLANES, :]))
        tail = g_new[tm - SUBLANES:tm, :]
        halo_s[c] = tail
        gk_ref[0] = tail
        activate(c, g_new, u_new, gs1, gs2)

    @pl.when(c == last)
    def _finish():
        x2 = x1_s[...] + jnp.dot(act_s[...], wd_ref[...], preferred_element_type=F32)
        o_ref[...] = _rms(x2, gf_ref[...]) if final else x2


def _ffn(x2d, ya, yb, yc, woa, wob, woc, g2, gf, wg, wu, wd, cw, cb, *, sample, blocks_per_seq=1, final=False,
         inj=None, tm=512):
    n = x2d.shape[0]
    tm = min(tm, n)
    nc = D_FF // FF_CHUNK
    rows = lambda w: pl.BlockSpec((tm, w), lambda i, c: (i, 0))
    const = lambda *shape: pl.BlockSpec(shape, lambda i, c: (0,) * len(shape))
    in_specs = [rows(D_MODEL), rows(ya.shape[1]), rows(yb.shape[1]), rows(yc.shape[1]),
                const(*woa.shape), const(*wob.shape), const(*woc.shape), const(1, D_MODEL), const(1, D_MODEL),
                pl.BlockSpec((D_MODEL, FF_CHUNK), lambda i, c: (0, c)),
                pl.BlockSpec((D_MODEL, FF_CHUNK), lambda i, c: (0, c)),
                const(D_FF, D_MODEL), const(CONV_W, D_FF), const(1, D_FF)]
    args = [x2d, ya, yb, yc, woa, wob, woc, g2.reshape(1, D_MODEL), gf.reshape(1, D_MODEL), wg, wu, wd, cw,
            cb.reshape(1, D_FF)]
    scratch = [pltpu.VMEM((tm, D_MODEL), F32), pltpu.VMEM((tm, D_MODEL), BF16), pltpu.VMEM((tm, D_FF), BF16)]
    if sample:
        in_specs += [pl.BlockSpec((tm, FF_CHUNK), lambda i, c: (i, c))] * 2
        args += list(inj)
        gk_spec = pl.BlockSpec((tm, FF_CHUNK), lambda i, c: (i, c))
        gk_shape = jax.ShapeDtypeStruct((n, D_FF), F32)
    else:
        scratch.append(pltpu.VMEM((nc, SUBLANES, FF_CHUNK), F32))
        gk_spec = pl.BlockSpec((1, SUBLANES, FF_CHUNK), lambda i, c: (i, 0, c))
        gk_shape = jax.ShapeDtypeStruct((n // tm, SUBLANES, D_FF), F32)
    return pl.pallas_call(
        functools.partial(_ffn_kernel, sample, blocks_per_seq, final),
        grid=(n // tm, nc),
        in_specs=in_specs,
        out_specs=[pl.BlockSpec((tm, D_MODEL), lambda i, c: (i, 0)), gk_spec],
        out_shape=[jax.ShapeDtypeStruct((n, D_MODEL), F32), gk_shape],
        scratch_shapes=scratch,
        compiler_params=_cparams(("arbitrary", "arbitrary")),
        name="ffn_sample" if sample else "ffn_prompt",
    )(*args)


N_PAGES = 16
PAST_LEN = N_PAGES * PAGE_SIZE
DEC_SEQ = SUBLANES


def _top_rank_lanes(vals, n):
    lane = lax.broadcasted_iota(jnp.int32, vals.shape, 1)
    rank = jnp.zeros(vals.shape, F32)
    for j in range(n):
        vj = vals[:, j:j + 1]
        beats = (vj > vals) | ((vj == vals) & (j < lane))
        rank = rank + beats.astype(F32)
    return rank


def _stack_heads(q, lane_heads):
    return jnp.concatenate([_head_mask(q, h) for h in lane_heads], axis=0)


def _pad_rows(x, rows=LANES):
    return jnp.concatenate([x, jnp.zeros((rows - x.shape[0], x.shape[1]), x.dtype)], axis=0)


def _softmax_pv(groups):
    ms = []
    for parts in groups:
        m = parts[0][0].max(axis=1, keepdims=True)
        for s, _, _ in parts[1:]:
            m = jnp.maximum(m, s.max(axis=1, keepdims=True))
        ms.append(m)
    ps, ls = [], []
    for parts, m in zip(groups, ms):
        p = [jnp.exp(s - m) for s, _, _ in parts]
        ls.append(sum(jnp.sum(x, axis=1, keepdims=True) for x in p))
        ps.append([x.astype(BF16) for x in p])
    outs = []
    for parts, p, l in zip(groups, ps, ls):
        o = 0.0
        for (_, v, feature_major), pb in zip(parts, p):
            if feature_major:
                o = o + lax.dot_general(pb, v, _NT, preferred_element_type=F32)
            else:
                o = o + jnp.dot(pb, v, preferred_element_type=F32)
        outs.append(o / l)
    return outs


def _lane_cat(pages, rows):
    return jnp.concatenate([pg[0, rows, :] for pg in pages], axis=1)


def _pair_out(o, r0, r1):
    lane = lax.broadcasted_iota(jnp.int32, (DEC_SEQ, LANES), 1)
    return jnp.where(lane < HEAD_DIM, o[r0:r0 + DEC_SEQ], o[r1:r1 + DEC_SEQ])


def _moba_s_kernel(pt_ref, q_ref, new_ref, bp_ref, bn_ref, *rest):
    pages = rest[:N_PAGES]
    o_ref = rest[N_PAGES]
    nblk = PAST_LEN // MOBA_BLOCK
    ppb = MOBA_BLOCK // PAGE_SIZE
    width = A_HEADS * HEAD_DIM
    q = q_ref[0]
    lane = lax.broadcasted_iota(jnp.int32, (2 * DEC_SEQ, LANES), 1)
    t_row = lax.broadcasted_iota(jnp.int32, (2 * DEC_SEQ, LANES), 0) % DEC_SEQ
    lane_sq = lax.broadcasted_iota(jnp.int32, (LANES, LANES), 1)
    groups = []
    for p in range(A_HEADS // 2):
        rows = slice(p * LANES, (p + 1) * LANES)
        cols = slice(p * LANES, (p + 1) * LANES)
        kmean_t = jnp.zeros((LANES, LANES), F32)
        for j in range(nblk):
            both = sum(pg[0, rows, :] for pg in pages[j * ppb:(j + 1) * ppb])
            kmean_t = jnp.where(lane_sq == j, jnp.sum(both, axis=1, keepdims=True) / MOBA_BLOCK, kmean_t)
        q2 = _stack_heads(q[:, cols], (0, 1))
        gate = jnp.dot(q2, kmean_t, precision=HIGHEST, preferred_element_type=F32)
        gm = jnp.where(lane < nblk, gate, NEG_INF)
        chosen = (_top_rank_lanes(gm, nblk) < MOBA_TOPK) & (lane < nblk)
        neg = jnp.where(chosen, 0.0, NEG_INF)
        q2b = q2.astype(BF16)
        ktb = _lane_cat(pages, rows).astype(BF16)
        vtb = _lane_cat(pages, slice(width + p * LANES, width + (p + 1) * LANES)).astype(BF16)
        s_past = jnp.dot(q2b, ktb, preferred_element_type=F32) + bp_ref[p]
        s_past = s_past + jnp.concatenate([jnp.broadcast_to(neg[:, j:j + 1], (2 * DEC_SEQ, MOBA_BLOCK))
                                           for j in range(nblk)], axis=1)
        knp = _pad_rows(new_ref[0, :, cols]).astype(BF16)
        vnp = _pad_rows(new_ref[0, :, width + p * LANES:width + (p + 1) * LANES]).astype(BF16)
        s_new = lax.dot_general(q2b, knp, _NT, preferred_element_type=F32) + bn_ref[p]
        s_new = jnp.where(lane <= t_row, s_new, NEG_INF)
        groups.append([(s_past, vtb, True), (s_new, vnp, False)])
    o_ref[0] = jnp.concatenate([_pair_out(o, 0, DEC_SEQ) for o in _softmax_pv(groups)], axis=1)


def _page_specs(rows):
    def spec(i):
        return pl.BlockSpec((1, rows, PAGE_SIZE), lambda b, pt, *_: (pt[b * N_PAGES + i], 0, 0))
    return [spec(i) for i in range(N_PAGES)]


def _moba_sample(pt, q_s, a_new, cache, bias_past, bias_new):
    nbt = q_s.shape[0]
    row = lambda w: pl.BlockSpec((1, DEC_SEQ, w), lambda b, pt: (b, 0, 0))
    const = lambda *shape: pl.BlockSpec(shape, lambda b, pt: (0,) * len(shape))
    gs = pltpu.PrefetchScalarGridSpec(
        num_scalar_prefetch=1, grid=(nbt,),
        in_specs=[row(Q_COLS), row(AKV_COLS), const(*bias_past.shape), const(*bias_new.shape)]
        + _page_specs(AKV_COLS),
        out_specs=row(A_HEADS * HEAD_DIM))
    return pl.pallas_call(
        _moba_s_kernel, grid_spec=gs,
        out_shape=jax.ShapeDtypeStruct((nbt, DEC_SEQ, A_HEADS * HEAD_DIM), F32),
        compiler_params=_cparams(("arbitrary",)), name="moba_sample",
    )(pt, q_s, a_new, bias_past, bias_new, *([cache] * N_PAGES))


def _split3(x):
    hi = x.astype(BF16)
    r1 = x - hi.astype(F32)
    mid = r1.astype(BF16)
    lo = (r1 - mid.astype(F32)).astype(BF16)
    return hi, mid, lo


def _fox_s_kernel(pt_ref, raw_ref, q_ref, new_ref, small_ref, cm_ref, *rest):
    pages = rest[:N_PAGES]
    lpages = rest[N_PAGES:2 * N_PAGES]
    o_ref = rest[2 * N_PAGES]
    x_s = rest[2 * N_PAGES + 1]
    b = pl.program_id(0)
    width = B_HEADS * HEAD_DIM
    qoff = A_HEADS * HEAD_DIM
    r16 = 2 * DEC_SEQ
    lane = lax.broadcasted_iota(jnp.int32, (r16, LANES), 1)
    t_row = lax.broadcasted_iota(jnp.int32, (r16, LANES), 0) % DEC_SEQ

    x_s[...] = jnp.zeros(x_s.shape, F32)
    for pg in range(N_PAGES):
        r = raw_ref[b * N_PAGES + pg] % SUBLANES
        for h in range(B_HEADS):
            x_s[h * N_PAGES + pg:h * N_PAGES + pg + 1, :] = lpages[pg][h, pl.ds(r, 1), :]
    hi, mid, lo = _split3(x_s[...])
    parts = jnp.dot(jnp.concatenate([hi, mid, lo], axis=0), cm_ref[0], preferred_element_type=F32)
    sfx = parts[0:LANES] + parts[LANES:2 * LANES] + parts[2 * LANES:3 * LANES]
    t_hi, t_mid, t_lo = _split3(sfx[:, LANES:])
    offp = jnp.dot(cm_ref[1, :, 0:LANES], jnp.concatenate([t_hi, t_mid, t_lo], axis=1),
                   preferred_element_type=F32)
    arow = sfx[:, 0:LANES] + (offp[:, 0:LANES] + offp[:, LANES:2 * LANES] + offp[:, 2 * LANES:3 * LANES])

    def decay_row(h):
        return jnp.concatenate([arow[h * N_PAGES + pg:h * N_PAGES + pg + 1, :] for pg in range(N_PAGES)], axis=1)

    lfn = small_ref[0]
    sub = lax.broadcasted_iota(jnp.int32, (DEC_SEQ, LANES), 0)
    lane8 = lax.broadcasted_iota(jnp.int32, (DEC_SEQ, LANES), 1)
    cs = lfn
    for sh in (1, 2, 4):
        cs = cs + jnp.where(sub >= sh, pltpu.roll(cs, sh, 0), 0.0)

    q = q_ref[0]
    groups = []
    for p in range(B_HEADS // 2):
        cols = slice(p * LANES, (p + 1) * LANES)
        q2b = _stack_heads(q[:, qoff + p * LANES:qoff + (p + 1) * LANES], (0, 1)).astype(BF16)
        heads = (2 * p, 2 * p + 1)
        a_rows = jnp.concatenate([jnp.broadcast_to(decay_row(h), (DEC_SEQ, PAST_LEN)) for h in heads], axis=0)
        d_col = jnp.concatenate([cs[:, h:h + 1] for h in heads], axis=0)
        d_row = jnp.concatenate(
            [jnp.broadcast_to(jnp.sum(jnp.where(sub <= lane8, jnp.broadcast_to(lfn[:, h:h + 1], (DEC_SEQ, LANES)),
                                                0.0), axis=0, keepdims=True), (DEC_SEQ, LANES))
             for h in heads], axis=0)
        ktb = _lane_cat(pages, cols).astype(BF16)
        vtb = _lane_cat(pages, slice(width + p * LANES, width + (p + 1) * LANES)).astype(BF16)
        s_past = jnp.dot(q2b, ktb, preferred_element_type=F32) + (a_rows + d_col)
        knp = _pad_rows(new_ref[0, :, cols]).astype(BF16)
        vnp = _pad_rows(new_ref[0, :, width + p * LANES:width + (p + 1) * LANES]).astype(BF16)
        s_new = lax.dot_general(q2b, knp, _NT, preferred_element_type=F32) + (d_col - d_row)
        s_new = jnp.where(lane <= t_row, s_new, NEG_INF)
        groups.append([(s_past, vtb, True), (s_new, vnp, False)])
    o_ref[0] = jnp.concatenate([_pair_out(o, 0, DEC_SEQ) for o in _softmax_pv(groups)], axis=1)


def _fox_consts():
    tok = np.arange(PAGE_SIZE)
    m = np.concatenate([tok[:, None] > tok[None, :], np.ones((PAGE_SIZE, PAGE_SIZE), bool)], axis=1)
    r = np.arange(LANES)
    u = (r[:, None] // N_PAGES == r[None, :] // N_PAGES) & (r[None, :] % N_PAGES > r[:, None] % N_PAGES)
    u = np.concatenate([u, np.zeros((LANES, LANES), bool)], axis=1)
    return jnp.asarray(np.stack([m, u]), BF16)


def _fox_sample(pt, pt_raw, q_s, b_new, small_s, cache, cache_logf, layer):
    nbt = q_s.shape[0]
    row = lambda w: pl.BlockSpec((1, DEC_SEQ, w), lambda b, pt, raw: (b, 0, 0))
    cm = _fox_consts()
    lspecs = [pl.BlockSpec((B_HEADS, SUBLANES, PAGE_SIZE),
                           functools.partial(lambda i, b, pt, raw: (layer, raw[b * N_PAGES + i] // SUBLANES, 0), i))
              for i in range(N_PAGES)]
    gs = pltpu.PrefetchScalarGridSpec(
        num_scalar_prefetch=2, grid=(nbt,),
        in_specs=[row(Q_COLS), row(BKV_COLS), row(SMALL_COLS),
                  pl.BlockSpec(cm.shape, lambda b, pt, raw: (0, 0, 0))] + _page_specs(BKV_COLS) + lspecs,
        out_specs=row(B_HEADS * HEAD_DIM),
        scratch_shapes=[pltpu.VMEM((LANES, PAGE_SIZE), F32)])
    return pl.pallas_call(
        _fox_s_kernel, grid_spec=gs,
        out_shape=jax.ShapeDtypeStruct((nbt, DEC_SEQ, B_HEADS * HEAD_DIM), F32),
        compiler_params=_cparams(("arbitrary",)), name="fox_sample",
    )(pt, pt_raw, q_s, b_new, small_s, cm, *([cache] * N_PAGES), *([cache_logf] * N_PAGES))


def _nsa_s_kernel(pt_ref, q_ref, new_ref, wnew_ref, gate_ref, win_ref, bc_ref, bp_ref, bn_ref, bw_ref,
                  cov_ref, exp_ref, pe_ref, w1_ref, w2_ref, *rest):
    pages = rest[:N_PAGES]
    o_ref = rest[N_PAGES]
    ctok = rest[N_PAGES + 1]
    ncp = PAST_LEN // CMP_STRIDE
    nbs_past = PAST_LEN // SLC_BLOCK
    qoff = (A_HEADS + B_HEADS) * HEAD_DIM
    nrow = C_HEADS * DEC_SEQ
    lane = lax.broadcasted_iota(jnp.int32, (nrow, LANES), 1)
    t_row = lax.broadcasted_iota(jnp.int32, (nrow, LANES), 0) % DEC_SEQ

    q = q_ref[0]
    q2 = jnp.concatenate([_head_mask(q[:, qoff + (s // 2) * LANES:qoff + (s // 2 + 1) * LANES], s % 2)
                          for s in range(C_HEADS)], axis=0)
    q2b = q2.astype(BF16)

    ktb = _lane_cat(pages, slice(2 * LANES, 3 * LANES)).astype(BF16)
    vtb = _lane_cat(pages, slice(3 * LANES, 4 * LANES)).astype(BF16)
    s_past = jnp.dot(q2b, ktb, preferred_element_type=F32) + bp_ref[...]
    knp = _pad_rows(new_ref[0, :, 2 * LANES:3 * LANES]).astype(BF16)
    vnp = _pad_rows(new_ref[0, :, 3 * LANES:4 * LANES]).astype(BF16)
    s_new = lax.dot_general(q2b, knp, _NT, preferred_element_type=F32) + bn_ref[...]
    wkt = win_ref[0, 0:LANES, :].astype(BF16)
    wvt = win_ref[0, LANES:2 * LANES, :].astype(BF16)
    wlane = lax.broadcasted_iota(jnp.int32, (nrow, WINDOW), 1)
    wt = lax.broadcasted_iota(jnp.int32, (nrow, WINDOW), 0) % DEC_SEQ
    s_w = jnp.dot(q2b, wkt, preferred_element_type=F32) + bw_ref[...]
    s_w = jnp.where(wlane > wt, s_w, NEG_INF)
    wkn = _pad_rows(wnew_ref[0, :, 0:LANES]).astype(BF16)
    wvn = _pad_rows(wnew_ref[0, :, LANES:2 * LANES]).astype(BF16)
    s_wn = lax.dot_general(q2b, wkn, _NT, preferred_element_type=F32) + bn_ref[...]
    s_wn = jnp.where(lane <= t_row, s_wn, NEG_INF)

    for i, pg in enumerate(pages):
        for kind in range(2):
            ctok[kind, i * PAGE_SIZE:(i + 1) * PAGE_SIZE, :] = pg[0, kind * LANES:(kind + 1) * LANES, :].T

    def rows_of(kind):
        return lambda l: ctok[kind, pl.ds(l, ncp, stride=CMP_STRIDE), :]

    kc = _compress_pair(rows_of(0), ncp, pe_ref, w1_ref, w2_ref, 0).astype(BF16)
    vc = _compress_pair(rows_of(1), ncp, pe_ref, w1_ref, w2_ref, 1).astype(BF16)

    sc = lax.dot_general(q2b, kc, _NT, preferred_element_type=F32) + bc_ref[...]
    sc = jnp.where(lane < ncp - 1, sc, NEG_INF)
    m = sc.max(axis=1, keepdims=True)
    e = jnp.exp(sc - m)
    pc = e / jnp.sum(e, axis=1, keepdims=True)
    o_cmp = jnp.dot(pc.astype(BF16), vc, preferred_element_type=F32)

    psum = [sum(pc[s * DEC_SEQ:(s + 1) * DEC_SEQ] for s in range(kvh, C_HEADS, 2)) for kvh in range(2)]
    imp = jnp.dot(jnp.concatenate(psum, axis=0), cov_ref[...], precision=HIGHEST, preferred_element_type=F32)
    lane16 = lax.broadcasted_iota(jnp.int32, (2 * DEC_SEQ, LANES), 1)
    forced = (lane16 == 0) | (lane16 == nbs_past) | (lane16 == nbs_past - 1)
    v = jnp.where(lane16 <= nbs_past, jnp.where(forced, FORCE_SCORE, imp), NEG_INF)
    chosen = (_top_rank_lanes(v, nbs_past + 1) < SLC_TOPK) & (lane16 <= nbs_past)
    chosen_f = jnp.where(chosen, 1.0, 0.0)
    mask16 = jnp.dot(chosen_f.astype(BF16), exp_ref[...], preferred_element_type=F32)
    mask_past = jnp.concatenate([mask16[(s % 2) * DEC_SEQ:(s % 2 + 1) * DEC_SEQ] for s in range(C_HEADS)], axis=0)
    mask_new = jnp.concatenate(
        [jnp.broadcast_to(chosen_f[(s % 2) * DEC_SEQ:(s % 2 + 1) * DEC_SEQ, nbs_past:nbs_past + 1],
                          (DEC_SEQ, LANES)) for s in range(C_HEADS)], axis=0)

    s_past = jnp.where(mask_past > 0.5, s_past, NEG_INF)
    s_new = jnp.where((lane <= t_row) & (mask_new > 0.5), s_new, NEG_INF)
    o_slc, o_win = _softmax_pv([[(s_past, vtb, True), (s_new, vnp, False)],
                                [(s_w, wvt, True), (s_wn, wvn, False)]])

    g = jax.nn.sigmoid(gate_ref[0])
    o = g[:, 0:1] * o_cmp + g[:, 1:2] * o_slc + g[:, 2:3] * o_win
    o_ref[0] = jnp.concatenate([_pair_out(o, 2 * m_ * DEC_SEQ, (2 * m_ + 1) * DEC_SEQ)
                                for m_ in range(C_HEADS // 2)], axis=1)


def _nsa_sample(pt, q_s, c_new, w_new, gates, win_state, layer, cache, tables, pe2, w1bd, w2bd):
    nbt = q_s.shape[0]
    bc, bp, bn, bw, cov, expand = tables
    row = lambda r, w: pl.BlockSpec((1, r, w), lambda b, pt: (b, 0, 0))
    win_spec = pl.BlockSpec((1, CWIN_COLS, WINDOW), lambda b, pt: (layer * nbt + b, 0, 0))
    const = lambda a: pl.BlockSpec(a.shape, lambda b, pt: (0,) * a.ndim)
    gs = pltpu.PrefetchScalarGridSpec(
        num_scalar_prefetch=1, grid=(nbt,),
        in_specs=[row(DEC_SEQ, Q_COLS), row(DEC_SEQ, CKV_COLS), row(DEC_SEQ, CWIN_COLS),
                  row(C_HEADS * DEC_SEQ, LANES), win_spec,
                  const(bc), const(bp), const(bn), const(bw), const(cov), const(expand),
                  const(pe2), const(w1bd), const(w2bd)] + _page_specs(CKV_COLS),
        out_specs=row(DEC_SEQ, C_HEADS * HEAD_DIM),
        scratch_shapes=[pltpu.VMEM((2, PAST_LEN, LANES), F32)])
    return pl.pallas_call(
        _nsa_s_kernel, grid_spec=gs,
        out_shape=jax.ShapeDtypeStruct((nbt, DEC_SEQ, C_HEADS * HEAD_DIM), F32),
        compiler_params=_cparams(("arbitrary",)), name="nsa_sample",
    )(pt, q_s, c_new, w_new, gates, win_state, bc, bp, bn, bw, cov, expand, pe2, w1bd, w2bd,
      *([cache] * N_PAGES))


def _bucket_of(dist):
    return _BUCKETS[np.clip(dist, 0, MAX_DISTANCE)].astype(np.int32)


_SAMPLE_KEY_POS = np.concatenate([
    np.arange(PAST_LEN),
    PAST_LEN + np.arange(LANES),
    PAST_LEN - WINDOW + np.arange(WINDOW),
    np.arange(PAST_LEN // CMP_STRIDE) * CMP_STRIDE + CMP_BLOCK - 1
])


def _bias_bucket_tables(t):
    k = np.arange(TQ)[:, None]
    q = np.arange(TQ)[None, :]
    ncp = t // CMP_STRIDE
    rel = np.arange(2 * ncp)[:, None] - ncp
    band = _bucket_of(q - (rel * CMP_STRIDE + CMP_BLOCK - 1))
    samp = _bucket_of(PAST_LEN + np.arange(DEC_SEQ)[:, None] - _SAMPLE_KEY_POS[None, :])
    masked = np.full((TQ, TQ), _MASKED_CODE, np.int32)
    own = np.where(q >= k, _bucket_of(q - k), masked)
    prev = _bucket_of(TQ + q - k)
    far = _bucket_of(np.full((TQ, TQ), 2 * TQ))
    far_win = np.where(k > q, far, masked)
    tiles3 = np.concatenate([own, prev, far], axis=0)
    tiles5 = np.concatenate([far_win, prev, own] + [masked] * (WINDOW // TQ), axis=0)
    return [jnp.asarray(a) for a in (tiles3, tiles5, band, samp)]


_MASKED_CODE = N_BUCKETS


def _bias_kernel(tab_ref, *refs):
    h = pl.program_id(0)
    n = len(refs) // 2
    for idx_ref, o_ref in zip(refs[:n], refs[n:]):
        idx = idx_ref[...]
        acc = jnp.full(idx.shape, NEG_INF, F32)
        for bkt in range(N_BUCKETS):
            acc = jnp.where(idx == bkt, tab_ref[bkt, h], acc)
        o_ref[0] = acc


def _bias_tables(rel_bias, t):
    idx = _bias_bucket_tables(t)
    nh = rel_bias.shape[1]
    return pl.pallas_call(
        _bias_kernel,
        grid=(nh,),
        in_specs=[pl.BlockSpec(memory_space=pltpu.SMEM)]
        + [pl.BlockSpec(a.shape, lambda h: (0, 0)) for a in idx],
        out_specs=[pl.BlockSpec((1,) + a.shape, lambda h: (h, 0, 0)) for a in idx],
        out_shape=[jax.ShapeDtypeStruct((nh,) + a.shape, F32) for a in idx],
        compiler_params=_cparams(("arbitrary",)),
        name="bias_tables",
    )(rel_bias, *idx)


def _sample_masks():
    ncp = PAST_LEN // CMP_STRIDE
    nbs = PAST_LEN // SLC_BLOCK
    ci = np.arange(ncp)[:, None]
    bj = np.arange(LANES)[None, :]
    cov = ((ci * CMP_STRIDE < bj * SLC_BLOCK + SLC_BLOCK) & (ci * CMP_STRIDE + CMP_BLOCK > bj * SLC_BLOCK)
           & (ci < ncp - 1) & (bj < nbs))
    expand = np.arange(LANES)[:, None] == (np.arange(PAST_LEN)[None, :] // SLC_BLOCK)
    return jnp.asarray(cov, F32), jnp.asarray(expand, BF16)


def _prep_compress(cmp_pe, cmp_w1, cmp_w2):
    pe2 = jnp.concatenate([cmp_pe, cmp_pe], axis=-1)
    z1 = jnp.zeros_like(cmp_w1)
    w1bd = jnp.concatenate([jnp.concatenate([cmp_w1, z1], axis=-1),
                            jnp.concatenate([z1, cmp_w1], axis=-1)], axis=-2)
    z2 = jnp.zeros_like(cmp_w2)
    w2bd = jnp.concatenate([jnp.concatenate([cmp_w2, z2], axis=-1),
                            jnp.concatenate([z2, cmp_w2], axis=-1)], axis=-2)
    return pe2, w1bd.astype(BF16), w2bd.astype(BF16)


def _cover_t(nbs, ncp):
    cstart = np.arange(ncp)[None, :] * CMP_STRIDE
    bstart = np.arange(nbs)[:, None] * SLC_BLOCK
    cov = (cstart < bstart + SLC_BLOCK) & (cstart + CMP_BLOCK > bstart) & (np.arange(ncp)[None, :] < ncp - 1)
    return jnp.asarray(cov, F32)


_SPLIT_SIZES = (256, 256, 256, 384, 384, 384, 6, 384, 128, 128, 128, 128, 128, 128, 18)


def _prep_w_in(w_in):
    offs = np.cumsum(_SPLIT_SIZES)[:-1].tolist()
    aq, ak, av, bq, bk, bv, bf, cq, cck, ccv, csk, csv, cwk, cwv, cg = jnp.split(w_in, offs, axis=-1)
    lead = cq.shape[:-1]
    cq = jnp.concatenate([cq[..., h * HEAD_DIM:(h + 1) * HEAD_DIM] for h in C_SLOT_HEADS], axis=-1)
    scale = HEAD_DIM ** -0.5
    pad = jnp.zeros(lead + (SMALL_COLS - B_HEADS - C_HEADS * N_GATES,), w_in.dtype)
    w = jnp.concatenate([aq * scale, bq * scale, cq * scale, ak, av, bk, bv,
                         cck, ccv, csk, csv, cwk, cwv, bf, cg, pad], axis=-1)
    return w.astype(BF16)


def _prep_small_bias(b_f, b_gate):
    pad = jnp.zeros(b_f.shape[:-1] + (SMALL_COLS - B_HEADS - C_HEADS * N_GATES,), F32)
    return jnp.concatenate([b_f, b_gate, pad], axis=-1)[:, None, :]


def _prep_w_out(w_out):
    wa = w_out[:, :A_HEADS * HEAD_DIM]
    wb = w_out[:, A_HEADS * HEAD_DIM:(A_HEADS + B_HEADS) * HEAD_DIM]
    wc = w_out[:, (A_HEADS + B_HEADS) * HEAD_DIM:]
    wc = jnp.concatenate([wc[:, h * HEAD_DIM:(h + 1) * HEAD_DIM] for h in C_SLOT_HEADS], axis=1)
    return wa.astype(BF16), wb.astype(BF16), wc.astype(BF16)


def kernel(x_prompt, x_sample, cache_a_kv, cache_b_kv, cache_b_logf, cache_c_kv, state_c_win, state_ffn_conv,
           page_table, rel_bias, ln1_g, w_in, b_f, b_gate, cmp_pe, cmp_w1, cmp_w2, w_out, ln2_g, w_gate, w_up,
           conv_w, conv_b, w_down, final_g):
    depth = w_in.shape[0]
    b, t, _ = x_prompt.shape
    nbt, ts, _ = x_sample.shape
    n_pool = cache_a_kv.shape[1]
    assert ts == DEC_SEQ and page_table.shape[1] == N_PAGES and cache_a_kv.shape[2] == PAGE_SIZE
    assert t % TQ == 0 and state_c_win.shape[2] == WINDOW and t >= WINDOW
    ffn_tm = 1024

    w_in_p = _prep_w_in(w_in)
    small_bias = _prep_small_bias(b_f, b_gate)
    pe2, w1bd, w2bd = _prep_compress(cmp_pe, cmp_w1, cmp_w2)
    woa, wob, woc = _prep_w_out(w_out)
    wg, wu, wd = w_gate.astype(BF16), w_up.astype(BF16), w_down.astype(BF16)

    tiles3, tiles5, band, samp = _bias_tables(rel_bias, t)
    c_slots = lambda x: jnp.stack([x[A_HEADS + h] for h in C_SLOT_HEADS], axis=0)
    tiles3_a, tiles3_c, tiles5_c, band_c = tiles3[:A_HEADS], c_slots(tiles3), c_slots(tiles5), c_slots(band)
    cover_t = _cover_t(t // SLC_BLOCK, t // CMP_STRIDE)
    o1, o2, o3 = PAST_LEN, PAST_LEN + LANES, PAST_LEN + LANES + WINDOW
    samp_a = samp[:A_HEADS].reshape(A_HEADS // 2, 2 * DEC_SEQ, -1)
    moba_tabs = (samp_a[..., :o1], samp_a[..., o1:o2])
    samp_c = c_slots(samp).reshape(C_HEADS * DEC_SEQ, -1)
    nsa_tabs = (samp_c[:, o3:], samp_c[:, :o1], samp_c[:, o1:o2], samp_c[:, o2:o3]) + _sample_masks()

    fm = lambda c: jnp.transpose(c, (0, 1, 3, 4, 5, 2))
    cache_a = fm(cache_a_kv).reshape(depth * n_pool, AKV_COLS, PAGE_SIZE)
    cache_b = fm(cache_b_kv).reshape(depth * n_pool, BKV_COLS, PAGE_SIZE)
    cache_c = fm(cache_c_kv).reshape(depth * n_pool, CKV_COLS, PAGE_SIZE)
    cache_lf = jnp.transpose(cache_b_logf, (0, 3, 1, 2)).reshape(depth * B_HEADS, n_pool, PAGE_SIZE)
    win_all = fm(state_c_win).reshape(depth * nbt, CWIN_COLS, WINDOW)
    pt_flat = page_table.reshape(-1).astype(jnp.int32)

    xp = x_prompt.reshape(b * t, D_MODEL)
    xs = x_sample.reshape(nbt * ts, D_MODEL)
    outs_p = [[] for _ in range(6)]
    outs_s = [[] for _ in range(6)]
    for l in range(depth):
        final = l == depth - 1
        q, akv, bkv, ckv, cwin, small = [o.reshape(b, t, -1) for o in
                                         _in_proj(xp, ln1_g[l], w_in_p[l], small_bias[l])]
        ya = _moba_prompt(q, akv, tiles3_a)
        yb = _fox_prompt(q, bkv, small)
        kc, vc = _compress_prompt(ckv, pe2[l], w1bd[l], w2bd[l])
        yc = _nsa_prompt(q, ckv, cwin, jnp.swapaxes(small, 1, 2), tiles3_c, tiles5_c, band_c, cover_t, kc, vc)
        xp, gk = _ffn(xp, ya.reshape(b * t, -1), yb.reshape(b * t, -1), yc.reshape(b * t, -1),
                      woa[l], wob[l], woc[l], ln2_g[l], final_g, wg[l], wu[l], wd[l], conv_w[l], conv_b[l],
                      sample=False, blocks_per_seq=t // ffn_tm, final=final, tm=ffn_tm)
        outs_p[0].append(akv.reshape(b, t, 2, A_HEADS, HEAD_DIM))
        outs_p[1].append(bkv.reshape(b, t, 2, B_HEADS, HEAD_DIM))
        outs_p[2].append(small[..., :B_HEADS])
        outs_p[3].append(ckv.reshape(b, t, 4, C_KV_HEADS, HEAD_DIM))
        outs_p[4].append(cwin[:, t - WINDOW:].reshape(b, WINDOW, 2, C_KV_HEADS, HEAD_DIM))
        outs_p[5].append(gk.reshape(b, t // ffn_tm, SUBLANES, D_FF)[:, -1, SUBLANES - (CONV_W - 1):])
        q, akv, bkv, ckv, cwin, small = [o.reshape(nbt, ts, -1) for o in
                                         _in_proj(xs, ln1_g[l], w_in_p[l], small_bias[l])]
        pt = pt_flat + l * n_pool
        ya = _moba_sample(pt, q, akv, cache_a, *moba_tabs)
        yb = _fox_sample(pt, pt_flat, q, bkv, small, cache_b, cache_lf, l)
        gl = small[..., B_HEADS:B_HEADS + C_HEADS * N_GATES].reshape(nbt, ts, C_HEADS, N_GATES)
        gl = jnp.concatenate([gl[:, :, h] for h in C_SLOT_HEADS], axis=1)
        gl = jnp.pad(gl, ((0, 0), (0, 0), (0, LANES - N_GATES)))
        yc = _nsa_sample(pt, q, ckv, cwin, gl, win_all, l, cache_c, nsa_tabs, pe2[l], w1bd[l], w2bd[l])
        prev = state_ffn_conv[l]
        zero = jnp.zeros((nbt, ts - 2, D_FF), F32)
        p1 = jnp.concatenate([prev[:, 1:2], zero, zero[:, :1]], axis=1).reshape(nbt * ts, D_FF)
        p2 = jnp.concatenate([prev, zero], axis=1).reshape(nbt * ts, D_FF)
        xs, g_s = _ffn(xs, ya.reshape(nbt * ts, -1), yb.reshape(nbt * ts, -1), yc.reshape(nbt * ts, -1),
                       woa[l], wob[l], woc[l], ln2_g[l], final_g, wg[l], wu[l], wd[l], conv_w[l], conv_b[l],
                       sample=True, final=final, inj=(p1, p2), tm=ffn_tm)
        outs_s[0].append(akv.reshape(nbt, ts, 2, A_HEADS, HEAD_DIM))
        outs_s[1].append(bkv.reshape(nbt, ts, 2, B_HEADS, HEAD_DIM))
        outs_s[2].append(small[..., :B_HEADS])
        outs_s[3].append(ckv.reshape(nbt, ts, 4, C_KV_HEADS, HEAD_DIM))
        outs_s[4].append(jnp.concatenate([state_c_win[l][:, ts:],
                                          cwin.reshape(nbt, ts, 2, C_KV_HEADS, HEAD_DIM)], axis=1))
        outs_s[5].append(g_s.reshape(nbt, ts, D_FF)[:, ts - (CONV_W - 1):])
    st = lambda rows: jnp.stack(rows, axis=0)
    return (xp.reshape(b, t, D_MODEL), xs.reshape(nbt, ts, D_MODEL),
            *[st(r) for r in outs_p], *[st(r) for r in outs_s])
```

```python
import functools
import math

import numpy as np
import jax
import jax.numpy as jnp
from jax import lax
from jax.experimental import pallas as pl
from jax.experimental.pallas import tpu as pltpu

F32 = jnp.float32
BF16 = jnp.bfloat16
HIGHEST = lax.Precision.HIGHEST

D_MODEL = 1024
HEAD_DIM = 64
A_HEADS = 4
B_HEADS = 6
C_HEADS = 6
C_KV_HEADS = 2
C_GROUP = 3
MOBA_BLOCK = 256
MOBA_TOPK = 3
CMP_BLOCK = 32
CMP_STRIDE = 16
CMP_HIDDEN = 128
SLC_BLOCK = 64
SLC_TOPK = 8
WINDOW = 512
N_GATES = 3
N_BUCKETS = 32
MAX_DISTANCE = 128
D_FF = 2816
CONV_W = 3
RMS_EPS = 1e-6
NEG_INF = -1e30
FORCE_SCORE = 1e6
PAGE_SIZE = 128

LANES = 128
SUBLANES = 8
TQ = 256
VMEM_LIMIT = 56 * 1024 * 1024

C_SLOT_HEADS = (0, 3, 1, 4, 2, 5)

Q_COLS = (A_HEADS + B_HEADS + C_HEADS) * HEAD_DIM
AKV_COLS = 2 * A_HEADS * HEAD_DIM
BKV_COLS = 2 * B_HEADS * HEAD_DIM
CKV_COLS = 4 * C_KV_HEADS * HEAD_DIM
CWIN_COLS = 2 * C_KV_HEADS * HEAD_DIM
SMALL_COLS = LANES
IN_COLS = Q_COLS + AKV_COLS + BKV_COLS + CKV_COLS + CWIN_COLS + SMALL_COLS

_NT = (((1,), (1,)), ((), ()))


def _bucket_table():
    n = np.arange(MAX_DISTANCE + 1)
    exact = N_BUCKETS // 2
    nf = np.maximum(n, 1).astype(np.float32)
    large = exact + (np.log(nf / exact) / math.log(MAX_DISTANCE / exact) * (N_BUCKETS - exact)).astype(np.int32)
    large = np.minimum(large, N_BUCKETS - 1)
    return np.where(n < exact, n, large).astype(np.int32)


_BUCKETS = _bucket_table()


def _cparams(sem, vmem=VMEM_LIMIT):
    return pltpu.CompilerParams(dimension_semantics=sem, vmem_limit_bytes=vmem)


def _gelu(x):
    return 0.5 * x * (1.0 + jnp.tanh(math.sqrt(2.0 / math.pi) * (x + 0.044715 * (x * x * x))))


def _log_sigmoid(x):
    return -(jnp.maximum(-x, 0.0) + jnp.log1p(jnp.exp(-jnp.abs(x))))


def _in_proj_kernel(x_ref, g_ref, w_ref, sb_ref, q_ref, akv_ref, bkv_ref, ckv_ref, cwin_ref, small_ref):
    x = x_ref[...]
    ms = jnp.mean(x * x, axis=-1, keepdims=True)
    h = (x * lax.rsqrt(ms + RMS_EPS)) * g_ref[...]
    hb = h.astype(BF16)
    off = 0
    for ref in (q_ref, akv_ref, bkv_ref, ckv_ref, cwin_ref):
        n = ref.shape[-1]
        ref[...] = jnp.dot(hb, w_ref[:, off:off + n], preferred_element_type=F32)
        off += n
    raw = jnp.dot(hb, w_ref[:, off:off + SMALL_COLS], preferred_element_type=F32) + sb_ref[...]
    lane = lax.broadcasted_iota(jnp.int32, raw.shape, 1)
    small_ref[...] = jnp.where(lane < B_HEADS, _log_sigmoid(raw), raw)


def _in_proj(x2d, g, w, small_bias, tm=256):
    n = x2d.shape[0]
    tm = min(tm, n)
    widths = (Q_COLS, AKV_COLS, BKV_COLS, CKV_COLS, CWIN_COLS, SMALL_COLS)
    return pl.pallas_call(
        _in_proj_kernel,
        grid=(n // tm,),
        in_specs=[pl.BlockSpec((tm, D_MODEL), lambda i: (i, 0)),
                  pl.BlockSpec((1, D_MODEL), lambda i: (0, 0)),
                  pl.BlockSpec((D_MODEL, IN_COLS), lambda i: (0, 0)),
                  pl.BlockSpec((1, SMALL_COLS), lambda i: (0, 0))],
        out_specs=[pl.BlockSpec((tm, c), lambda i: (i, 0)) for c in widths],
        out_shape=[jax.ShapeDtypeStruct((n, c), F32) for c in widths],
        compiler_params=_cparams(("parallel",)),
        name="in_proj",
    )(x2d, g.reshape(1, D_MODEL), w, small_bias)


def _flash_init(s, vt):
    m = jnp.max(s, axis=0, keepdims=True)
    p = jnp.exp(s - m)
    l = jnp.sum(p, axis=0, keepdims=True)
    acc = jnp.dot(vt, p.astype(BF16), preferred_element_type=F32)
    return m, l, acc


def _attend_t(scs, vts):
    ones = jnp.ones((2 * SUBLANES, scs[0].shape[0]), BF16)
    ms = [jnp.max(s, axis=0, keepdims=True) for s in scs]
    ps = [jnp.exp(s - m).astype(BF16) for s, m in zip(scs, ms)]
    accs = [jnp.dot(jnp.concatenate([vt, ones], axis=0), p, preferred_element_type=F32) for vt, p in zip(vts, ps)]
    return [a[0:HEAD_DIM] / a[HEAD_DIM:HEAD_DIM + 1] for a in accs]


def _flash_step(s, vt, m, l, acc):
    m_new = jnp.maximum(m, jnp.max(s, axis=0, keepdims=True))
    a = jnp.exp(m - m_new)
    p = jnp.exp(s - m_new)
    l = a * l + jnp.sum(p, axis=0, keepdims=True)
    acc = a * acc + jnp.dot(vt, p.astype(BF16), preferred_element_type=F32)
    return m_new, l, acc


def _flash_pipeline(n_tiles, score_fn, vt_fn, sbuf, heads=2, first=None):
    for h in range(heads):
        sbuf[h] = score_fn(h, 0) if first is None else first(h)
    pad = 2 * SUBLANES
    ones = jnp.ones((pad, TQ), BF16)
    init = []
    for h in range(heads):
        init += [jnp.full((1, TQ), NEG_INF, F32), jnp.zeros((HEAD_DIM + pad, TQ), F32)]

    def body(i, st):
        cur = [sbuf[h] for h in range(heads)]
        nxt = [score_fn(h, jnp.minimum(i + 1, n_tiles - 1)) for h in range(heads)]
        out = []
        for h in range(heads):
            m, acc = st[2 * h:2 * h + 2]
            m_new = jnp.maximum(m, jnp.max(cur[h], axis=0, keepdims=True))
            p = jnp.exp(cur[h] - m_new).astype(BF16)
            vt1 = jnp.concatenate([vt_fn(h, i), ones], axis=0)
            acc = jnp.exp(m - m_new) * acc + jnp.dot(vt1, p, preferred_element_type=F32)
            out.extend((m_new, acc))
        for h in range(heads):
            sbuf[h] = nxt[h]
        return tuple(out)

    st = lax.fori_loop(0, n_tiles, body, tuple(init))
    out = []
    for h in range(heads):
        acc = st[2 * h + 1]
        out.extend((st[2 * h], acc[HEAD_DIM:HEAD_DIM + 1, :], acc[0:HEAD_DIM, :]))
    return out


def _top_rank(vals, row):
    rank = jnp.zeros(vals.shape, F32)
    for j in range(vals.shape[0]):
        vj = vals[j:j + 1, :]
        beats = (vj > vals) | ((vj == vals) & (j < row))
        rank = rank + beats.astype(F32)
    return rank


def _head_mask(q, h):
    lane = lax.broadcasted_iota(jnp.int32, (1, LANES), 1)
    return jnp.where((lane // HEAD_DIM) == h, q, 0.0)


def _moba_p_kernel(q0_ref, q1_ref, k_ref, v_ref, b3_ref, o_ref, kb, vtb, kmean, sel, sbuf):
    qi = pl.program_id(1)
    nb = kmean.shape[0]

    @pl.when(qi == 0)
    def _prep():
        k = k_ref[0]
        kb[...] = k.astype(BF16)
        vtb[...] = v_ref[0].T.astype(BF16)
        for j in range(nb):
            kmean[j:j + 1, :] = jnp.mean(k[j * TQ:(j + 1) * TQ, :], axis=0, keepdims=True)

    rowj = lax.broadcasted_iota(jnp.int32, (nb, TQ), 0)
    qbs = []
    for h in range(A_HEADS):
        pair = slice((h // 2) * LANES, (h // 2 + 1) * LANES)
        qh = _head_mask((q0_ref, q1_ref)[h // 2][0], h % 2)
        qbs.append(qh.astype(BF16))
        g = lax.dot_general(kmean[:, pair], qh, _NT, precision=HIGHEST, preferred_element_type=F32)
        gm = jnp.where(rowj < qi, g, NEG_INF)
        chosen = ((_top_rank(gm, rowj) < MOBA_TOPK) & (rowj < qi)) | (rowj == qi)
        sel[h] = jnp.where(chosen, 0.0, NEG_INF)

    def score_fn(h, i):
        j = qi - i
        kblk = kb[pl.ds(pl.multiple_of(j * TQ, TQ), TQ), (h // 2) * LANES:(h // 2 + 1) * LANES]
        bias = b3_ref[h, pl.ds(pl.multiple_of(jnp.minimum(i, 2) * TQ, TQ), TQ), :]
        return lax.dot_general(kblk, qbs[h], _NT, preferred_element_type=F32) + bias + sel[h, pl.ds(j, 1), :]

    def vt_fn(h, i):
        return vtb[h * HEAD_DIM:(h + 1) * HEAD_DIM, pl.ds(pl.multiple_of((qi - i) * TQ, TQ), TQ)]

    st = _flash_pipeline(qi + 1, score_fn, vt_fn, sbuf, heads=A_HEADS)
    o = jnp.concatenate([st[3 * h + 2] / st[3 * h + 1] for h in range(A_HEADS)], axis=0)
    o_ref[0] = o.T


def _moba_prompt(q_all, a_kv, tiles3):
    b, t, _ = q_all.shape
    width = A_HEADS * HEAD_DIM
    nb = t // TQ
    return pl.pallas_call(
        _moba_p_kernel,
        grid=(b, nb),
        in_specs=[pl.BlockSpec((1, TQ, LANES), lambda bi, qi: (bi, qi, 0)),
                  pl.BlockSpec((1, TQ, LANES), lambda bi, qi: (bi, qi, 1)),
                  pl.BlockSpec((1, t, width), lambda bi, qi: (bi, 0, 0)),
                  pl.BlockSpec((1, t, width), lambda bi, qi: (bi, 0, 1)),
                  pl.BlockSpec((A_HEADS, 3 * TQ, TQ), lambda bi, qi: (0, 0, 0))],
        out_specs=pl.BlockSpec((1, TQ, width), lambda bi, qi: (bi, qi, 0)),
        out_shape=jax.ShapeDtypeStruct((b, t, width), F32),
        scratch_shapes=[pltpu.VMEM((t, width), BF16), pltpu.VMEM((width, t), BF16),
                        pltpu.VMEM((nb, width), F32), pltpu.VMEM((A_HEADS, nb, TQ), F32),
                        pltpu.VMEM((A_HEADS, TQ, TQ), F32)],
        compiler_params=_cparams(("parallel", "arbitrary")),
        name="moba_prompt",
    )(q_all, q_all, a_kv, a_kv, tiles3)


def _fox_p_kernel(q0_ref, q1_ref, q2_ref, k_ref, v_ref, lf_ref, o_ref, kb, vtb, ckb, sbuf):
    qi = pl.program_id(1)
    t = kb.shape[0]
    nb = t // TQ

    @pl.when(qi == 0)
    def _prep():
        kb[...] = k_ref[0].astype(BF16)
        vtb[...] = v_ref[0].T.astype(BF16)
        r = lax.broadcasted_iota(jnp.int32, (TQ, TQ), 0)
        c = lax.broadcasted_iota(jnp.int32, (TQ, TQ), 1)
        tri = (c <= r).astype(BF16)
        sr = lax.broadcasted_iota(jnp.int32, (LANES, LANES), 0)
        parts = _split3(lf_ref[0])
        for h in range(B_HEADS):
            pick = (sr == h).astype(BF16)
            sel3 = jnp.concatenate([jnp.dot(x, pick, preferred_element_type=F32).astype(BF16) for x in parts],
                                   axis=1)
            carry = jnp.zeros((1, LANES), F32)
            for j in range(nb):
                c3 = jnp.dot(tri, sel3[j * TQ:(j + 1) * TQ, :], preferred_element_type=F32)
                blk = c3[:, 0:LANES] + c3[:, LANES:2 * LANES] + c3[:, 2 * LANES:3 * LANES] + carry
                ckb[h, j * TQ:(j + 1) * TQ, :] = blk
                carry = blk[TQ - 1:TQ, :]

    q_refs = (q0_ref, q1_ref, q2_ref)
    qbs = [_head_mask(q_refs[h // 2][0], h % 2).astype(BF16) for h in range(B_HEADS)]
    kk = lax.broadcasted_iota(jnp.int32, (TQ, LANES), 0)
    for h in range(B_HEADS):
        own = ckb[h, pl.ds(pl.multiple_of(qi * TQ, TQ), TQ), :]
        for half in range(TQ // LANES):
            qq = lax.broadcasted_iota(jnp.int32, (TQ, LANES), 1) + half * LANES
            ckb[h, t + half * TQ:t + (half + 1) * TQ, :] = jnp.where(kk <= qq, own, -NEG_INF)

    def k_tile(h, j):
        return kb[pl.ds(pl.multiple_of(j * TQ, TQ), TQ), (h // 2) * LANES:(h // 2 + 1) * LANES]

    def score_fn(h, i):
        j = qi - i
        s = lax.dot_general(k_tile(h, j), qbs[h], _NT, preferred_element_type=F32)
        c = ckb[h, pl.ds(pl.multiple_of(j * TQ, TQ), TQ), :]
        return s - jnp.concatenate([c, c], axis=1)

    def score_own(h):
        s = lax.dot_general(k_tile(h, qi), qbs[h], _NT, preferred_element_type=F32)
        return s - jnp.concatenate([ckb[h, t:t + TQ, :], ckb[h, t + TQ:t + 2 * TQ, :]], axis=1)

    def vt_fn(h, i):
        return vtb[h * HEAD_DIM:(h + 1) * HEAD_DIM, pl.ds(pl.multiple_of((qi - i) * TQ, TQ), TQ)]

    st = _flash_pipeline(qi + 1, score_fn, vt_fn, sbuf, heads=B_HEADS, first=score_own)
    o = jnp.concatenate([st[3 * h + 2] / st[3 * h + 1] for h in range(B_HEADS)], axis=0)
    o_ref[0] = o.T


def _fox_prompt(q_all, b_kv, small):
    b, t, _ = q_all.shape
    width = B_HEADS * HEAD_DIM
    qoff = A_HEADS // 2
    nb = t // TQ
    qspec = lambda m: pl.BlockSpec((1, TQ, LANES), lambda bi, qi: (bi, qi, qoff + m))
    return pl.pallas_call(
        _fox_p_kernel,
        grid=(b, nb),
        in_specs=[qspec(0), qspec(1), qspec(2),
                  pl.BlockSpec((1, t, width), lambda bi, qi: (bi, 0, 0)),
                  pl.BlockSpec((1, t, width), lambda bi, qi: (bi, 0, 1)),
                  pl.BlockSpec((1, t, LANES), lambda bi, qi: (bi, 0, 0))],
        out_specs=pl.BlockSpec((1, TQ, width), lambda bi, qi: (bi, qi, 0)),
        out_shape=jax.ShapeDtypeStruct((b, t, width), F32),
        scratch_shapes=[pltpu.VMEM((t, width), BF16), pltpu.VMEM((width, t), BF16),
                        pltpu.VMEM((B_HEADS, t + 2 * TQ, LANES), F32), pltpu.VMEM((B_HEADS, TQ, TQ), F32)],
        compiler_params=_cparams(("parallel", "arbitrary")),
        name="fox_prompt",
    )(q_all, q_all, q_all, b_kv, b_kv, small)


def _compress_pair(rows_of, ncp, pe_ref, w1_ref, w2_ref, kind):
    half = CMP_BLOCK // 2
    acc_a = jnp.zeros((ncp, 2 * CMP_HIDDEN), F32)
    acc_b = jnp.zeros((ncp, 2 * CMP_HIDDEN), F32)
    for l in range(0, half, 2):
        r0, r1 = rows_of(l), rows_of(l + 1)
        xa = jnp.concatenate([(r0 + pe_ref[kind, l:l + 1, :]).astype(BF16),
                              (r1 + pe_ref[kind, l + 1:l + 2, :]).astype(BF16)], axis=1)
        xb = jnp.concatenate([(r0 + pe_ref[kind, half + l:half + l + 1, :]).astype(BF16),
                              (r1 + pe_ref[kind, half + l + 1:half + l + 2, :]).astype(BF16)], axis=1)
        wa = jnp.concatenate([w1_ref[kind, l], w1_ref[kind, l + 1]], axis=0)
        wb = jnp.concatenate([w1_ref[kind, half + l], w1_ref[kind, half + l + 1]], axis=0)
        acc_a = acc_a + jnp.dot(xa, wa, preferred_element_type=F32)
        acc_b = acc_b + jnp.dot(xb, wb, preferred_element_type=F32)
    pre = acc_a + pltpu.roll(acc_b, ncp - 1, 0)
    hdn = _gelu(pre).astype(BF16)
    return jnp.dot(hdn, w2_ref[kind], preferred_element_type=F32)


def _compress_p_kernel(ck_ref, cv_ref, pe_ref, w1_ref, w2_ref, kc_ref, vc_ref):
    ncp = kc_ref.shape[1]

    def rows_of(ref):
        return lambda l: ref[0, pl.ds(l, ncp, stride=CMP_STRIDE), :]

    kc_ref[0] = _compress_pair(rows_of(ck_ref), ncp, pe_ref, w1_ref, w2_ref, 0).astype(BF16)
    vc_ref[0] = _compress_pair(rows_of(cv_ref), ncp, pe_ref, w1_ref, w2_ref, 1)


def _compress_prompt(c_kv, pe2, w1bd, w2bd):
    b, t, _ = c_kv.shape
    ncp = t // CMP_STRIDE
    const = lambda *shape: pl.BlockSpec(shape, lambda bi: (0,) * len(shape))
    return pl.pallas_call(
        _compress_p_kernel,
        grid=(b,),
        in_specs=[pl.BlockSpec((1, t, LANES), lambda bi: (bi, 0, 0)),
                  pl.BlockSpec((1, t, LANES), lambda bi: (bi, 0, 1)),
                  const(2, CMP_BLOCK, LANES), const(2, CMP_BLOCK, LANES, 2 * CMP_HIDDEN),
                  const(2, 2 * CMP_HIDDEN, LANES)],
        out_specs=[pl.BlockSpec((1, ncp, LANES), lambda bi: (bi, 0, 0)),
                   pl.BlockSpec((1, ncp, LANES), lambda bi: (bi, 0, 0))],
        out_shape=[jax.ShapeDtypeStruct((b, ncp, LANES), BF16), jax.ShapeDtypeStruct((b, ncp, LANES), F32)],
        compiler_params=_cparams(("parallel",)),
        name="compress_prompt",
    )(c_kv, c_kv, pe2, w1bd, w2bd)


def _nsa_p_kernel(q0_ref, q1_ref, q2_ref, ckv_ref, cwin_ref, gt_ref, t3_ref, w5_ref, bc_ref, cov_ref,
                  kcb, vc_ref, o_ref, vctb, skb, svtb, wkb, wvtb, selsc, sbuf):
    qi = pl.program_id(1)
    t = skb.shape[0]
    ncp = kcb.shape[1]
    nbs = selsc.shape[1]
    per_tile = TQ // SLC_BLOCK

    @pl.when(qi == 0)
    def _prep():
        vctb[...] = vc_ref[0].T.astype(BF16)
        skb[...] = ckv_ref[0, :, 0:LANES].astype(BF16)
        svtb[...] = ckv_ref[0, :, LANES:2 * LANES].T.astype(BF16)
        wkb[...] = cwin_ref[0, :, 0:LANES].astype(BF16)
        wvtb[...] = cwin_ref[0, :, LANES:2 * LANES].T.astype(BF16)

    q_refs = (q0_ref, q1_ref, q2_ref)
    qpos = qi * TQ + lax.broadcasted_iota(jnp.int32, (1, TQ), 1)
    ci = lax.broadcasted_iota(jnp.int32, (ncp, TQ), 0)
    cok = (qpos >= ci * CMP_STRIDE + (CMP_BLOCK - 1)) & (ci < ncp - 1)

    qbs = [_head_mask(q_refs[s // 2][0], s % 2).astype(BF16) for s in range(C_HEADS)]

    wn = WINDOW + TQ
    wstart = pl.multiple_of(jnp.maximum(qi - WINDOW // TQ, 0) * TQ, TQ)
    woff = pl.multiple_of((WINDOW // TQ - jnp.minimum(qi, WINDOW // TQ)) * TQ, TQ)
    o_win = _attend_t(
        [lax.dot_general(wkb[pl.ds(wstart, wn), :], qbs[s], _NT, preferred_element_type=F32)
         + w5_ref[s, pl.ds(woff, wn), :] for s in range(C_HEADS)],
        [wvtb[(s % 2) * HEAD_DIM:(s % 2 + 1) * HEAD_DIM, pl.ds(wstart, wn)] for s in range(C_HEADS)])

    band0 = pl.multiple_of(ncp - qi * (TQ // CMP_STRIDE), TQ // CMP_STRIDE)
    scs = [jnp.where(cok, lax.dot_general(kcb[0], qbs[s], _NT, preferred_element_type=F32)
                     + bc_ref[s, pl.ds(band0, ncp), :], NEG_INF) for s in range(C_HEADS)]
    pcs = []
    for sc in scs:
        m = jnp.max(sc, axis=0, keepdims=True)
        e = jnp.where(cok, jnp.exp(sc - m), 0.0)
        den = jnp.sum(e, axis=0, keepdims=True)
        pcs.append(e * jnp.where(den > 0.0, 1.0 / den, 0.0))
    psum = [pcs[kvh] + pcs[kvh + 2] + pcs[kvh + 4] for kvh in range(2)]
    ocmp = [jnp.dot(vctb[(s % 2) * HEAD_DIM:(s % 2 + 1) * HEAD_DIM, :], pcs[s].astype(BF16),
                    preferred_element_type=F32) for s in range(C_HEADS)]

    jj = lax.broadcasted_iota(jnp.int32, (nbs, TQ), 0)
    own = qpos // SLC_BLOCK
    forced = (jj == 0) | (jj == own) | (jj == own - 1)
    for kvh in range(2):
        imp = jnp.dot(cov_ref[...], psum[kvh], precision=HIGHEST, preferred_element_type=F32)
        v = jnp.where(jj <= own, jnp.where(forced, FORCE_SCORE, imp), NEG_INF)
        chosen = (_top_rank(v, jj) < SLC_TOPK) & (jj <= own)
        selsc[kvh] = jnp.where(chosen, 0.0, NEG_INF)

    def sel_rows(kvh, j):
        rows = [jnp.broadcast_to(selsc[kvh, pl.ds(j * per_tile + r, 1), :], (SLC_BLOCK, TQ))
                for r in range(per_tile)]
        return jnp.concatenate(rows, axis=0)

    def score_fn(s, i):
        j = qi - i
        kblk = skb[pl.ds(pl.multiple_of(j * TQ, TQ), TQ), :]
        bias = t3_ref[s, pl.ds(pl.multiple_of(jnp.minimum(i, 2) * TQ, TQ), TQ), :]
        return lax.dot_general(kblk, qbs[s], _NT, preferred_element_type=F32) + bias + sel_rows(s % 2, j)

    def vt_fn(s, i):
        kvh = s % 2
        return svtb[kvh * HEAD_DIM:(kvh + 1) * HEAD_DIM, pl.ds(pl.multiple_of((qi - i) * TQ, TQ), TQ)]

    st_slc = _flash_pipeline(qi + 1, score_fn, vt_fn, sbuf, heads=C_HEADS)

    for pair in range(C_HEADS // 2):
        outs = []
        for s in (2 * pair, 2 * pair + 1):
            o_slc = st_slc[3 * s + 2] / st_slc[3 * s + 1]
            grow = B_HEADS + C_SLOT_HEADS[s] * N_GATES
            g = [jax.nn.sigmoid(gt_ref[0, grow + n_:grow + n_ + 1, :]) for n_ in range(N_GATES)]
            outs.append(g[0] * ocmp[s] + g[1] * o_slc + g[2] * o_win[s])
        o_ref[0, :, pair * LANES:(pair + 1) * LANES] = jnp.concatenate(outs, axis=0).T


def _nsa_prompt(q_all, c_kv, c_win, small_t, tiles3, tiles5, bc, cover_t, kc, vct):
    b, t, _ = q_all.shape
    assert t >= WINDOW + TQ
    nq = t // TQ
    ncp = t // CMP_STRIDE
    nbs = t // SLC_BLOCK
    qoff = (A_HEADS + B_HEADS) // 2
    const = lambda *shape: pl.BlockSpec(shape, lambda bi, qi: (0,) * len(shape))
    qspec = lambda m: pl.BlockSpec((1, TQ, LANES), lambda bi, qi: (bi, qi, qoff + m))
    return pl.pallas_call(
        _nsa_p_kernel,
        grid=(b, nq),
        in_specs=[qspec(0), qspec(1), qspec(2),
                  pl.BlockSpec((1, t, 2 * LANES), lambda bi, qi: (bi, 0, 1)),
                  pl.BlockSpec((1, t, CWIN_COLS), lambda bi, qi: (bi, 0, 0)),
                  pl.BlockSpec((1, SMALL_COLS, TQ), lambda bi, qi: (bi, 0, qi)),
                  const(C_HEADS, 3 * TQ, TQ),
                  const(C_HEADS, WINDOW + 3 * TQ, TQ),
                  const(C_HEADS, 2 * ncp, TQ),
                  const(nbs, ncp),
                  pl.BlockSpec((1, ncp, LANES), lambda bi, qi: (bi, 0, 0)),
                  pl.BlockSpec((1, ncp, LANES), lambda bi, qi: (bi, 0, 0))],
        out_specs=pl.BlockSpec((1, TQ, C_HEADS * HEAD_DIM), lambda bi, qi: (bi, qi, 0)),
        out_shape=jax.ShapeDtypeStruct((b, t, C_HEADS * HEAD_DIM), F32),
        scratch_shapes=[pltpu.VMEM((LANES, ncp), BF16),
                        pltpu.VMEM((t, LANES), BF16), pltpu.VMEM((LANES, t), BF16),
                        pltpu.VMEM((t, LANES), BF16), pltpu.VMEM((LANES, t), BF16),
                        pltpu.VMEM((2, nbs, TQ), F32), pltpu.VMEM((C_HEADS, TQ, TQ), F32)],
        compiler_params=_cparams(("parallel", "arbitrary")),
        name="nsa_prompt",
    )(q_all, q_all, q_all, c_kv, c_win, small_t, tiles3, tiles5, bc, cover_t, kc, vct)


FF_CHUNK = 256
FFN_SLAB = 256


def _rms(x, g):
    ms = jnp.mean(x * x, axis=-1, keepdims=True)
    return (x * lax.rsqrt(ms + RMS_EPS)) * g


def _ffn_kernel(sample, blocks_per_seq, final, x_ref, ya_ref, yb_ref, yc_ref, woa_ref, wob_ref, woc_ref,
                g2_ref, gf_ref, wg_ref, wu_ref, wd_ref, cw_ref, cb_ref, *rest):
    if sample:
        p1_ref, p2_ref, o_ref, gk_ref, x1_s, h2_s, act_s = rest
    else:
        o_ref, gk_ref, x1_s, h2_s, act_s, halo_s = rest
    i = pl.program_id(0)
    c = pl.program_id(1)
    last = pl.num_programs(1) - 1
    tm = x_ref.shape[0]

    col = pl.ds(pl.multiple_of(c * FF_CHUNK, FF_CHUNK), FF_CHUNK)

    def activate(rows, g, u, gs1, gs2):
        conv = cb_ref[:, col] + cw_ref[0:1, col] * gs2
        conv = conv + cw_ref[1:2, col] * gs1
        conv = conv + cw_ref[2:3, col] * g
        act_s[rows, col] = (_gelu(conv) * u).astype(BF16)

    @pl.when(c == 0)
    def _attn_out():
        y = (jnp.dot(ya_ref[...].astype(BF16), woa_ref[...], preferred_element_type=F32)
             + jnp.dot(yb_ref[...].astype(BF16), wob_ref[...], preferred_element_type=F32)
             + jnp.dot(yc_ref[...].astype(BF16), woc_ref[...], preferred_element_type=F32))
        x1 = x_ref[...] + y
        x1_s[...] = x1
        h2_s[...] = _rms(x1, g2_ref[...]).astype(BF16)

    rs = FFN_SLAB if tm % FFN_SLAB == 0 else tm
    slabs = [slice(r0, r0 + rs) for r0 in range(0, tm, rs)]
    gs = [jnp.dot(h2_s[r, :], wg_ref[...], preferred_element_type=F32) for r in slabs]
    us = [jnp.dot(h2_s[r, :], wu_ref[...], preferred_element_type=F32) for r in slabs]
    row = lax.broadcasted_iota(jnp.int32, (rs, FF_CHUNK), 0)
    if sample:
        t = row % SUBLANES
        for r, g, u in zip(slabs, gs, us):
            gs1 = jnp.where(t >= 1, pltpu.roll(g, 1, 0), p1_ref[r, :])
            gs2 = jnp.where(t >= 2, pltpu.roll(g, 2, 0), p2_ref[r, :])
            gk_ref[r, :] = g
            activate(r, g, u, gs1, gs2)
    else:
        halo = jnp.where(i % blocks_per_seq == 0, 0.0, halo_s[c])
        for r, g, u in zip(slabs, gs, us):
            gs1 = jnp.where(row >= 1, pltpu.roll(g, 1, 0), halo[SUBLANES - 1:SUBLANES, :])
            gs2 = jnp.where(row >= 2, pltpu.roll(g, 2, 0),
                            jnp.where(row == 0, halo[SUBLANES - 2:SUBLANES - 1, :], halo[SUBLANES - 1:SUBLANES, :]))
            activate(r, g, u, gs1, gs2)
            halo = g[rs - SUBLANES:rs, :]
        halo_s[c] = halo
        gk_ref[0] = halo

    @pl.when(c == last)
    def _finish():
        x2 = x1_s[...] + jnp.dot(act_s[...], wd_ref[...], preferred_element_type=F32)
        o_ref[...] = _rms(x2, gf_ref[...]) if final else x2


def _ffn(x2d, ya, yb, yc, woa, wob, woc, g2, gf, wg, wu, wd, cw, cb, *, sample, blocks_per_seq=1, final=False,
         inj=None, tm=512):
    n = x2d.shape[0]
    tm = min(tm, n)
    nc = D_FF // FF_CHUNK
    rows = lambda w: pl.BlockSpec((tm, w), lambda i, c: (i, 0))
    const = lambda *shape: pl.BlockSpec(shape, lambda i, c: (0,) * len(shape))
    in_specs = [rows(D_MODEL), rows(ya.shape[1]), rows(yb.shape[1]), rows(yc.shape[1]),
                const(*woa.shape), const(*wob.shape), const(*woc.shape), const(1, D_MODEL), const(1, D_MODEL),
                pl.BlockSpec((D_MODEL, FF_CHUNK), lambda i, c: (0, c)),
                pl.BlockSpec((D_MODEL, FF_CHUNK), lambda i, c: (0, c)),
                const(D_FF, D_MODEL), const(CONV_W, D_FF), const(1, D_FF)]
    args = [x2d, ya, yb, yc, woa, wob, woc, g2.reshape(1, D_MODEL), gf.reshape(1, D_MODEL), wg, wu, wd, cw,
            cb.reshape(1, D_FF)]
    scratch = [pltpu.VMEM((tm, D_MODEL), F32), pltpu.VMEM((tm, D_MODEL), BF16), pltpu.VMEM((tm, D_FF), BF16)]
    if sample:
        in_specs += [pl.BlockSpec((tm, FF_CHUNK), lambda i, c: (i, c))] * 2
        args += list(inj)
        gk_spec = pl.BlockSpec((tm, FF_CHUNK), lambda i, c: (i, c))
        gk_shape = jax.ShapeDtypeStruct((n, D_FF), F32)
    else:
        scratch.append(pltpu.VMEM((nc, SUBLANES, FF_CHUNK), F32))
        gk_spec = pl.BlockSpec((1, SUBLANES, FF_CHUNK), lambda i, c: (i, 0, c))
        gk_shape = jax.ShapeDtypeStruct((n // tm, SUBLANES, D_FF), F32)
    return pl.pallas_call(
        functools.partial(_ffn_kernel, sample, blocks_per_seq, final),
        grid=(n // tm, nc),
        in_specs=in_specs,
        out_specs=[pl.BlockSpec((tm, D_MODEL), lambda i, c: (i, 0)), gk_spec],
        out_shape=[jax.ShapeDtypeStruct((n, D_MODEL), F32), gk_shape],
        scratch_shapes=scratch,
        compiler_params=_cparams(("arbitrary", "arbitrary")),
        name="ffn_sample" if sample else "ffn_prompt",
    )(*args)


N_PAGES = 16
PAST_LEN = N_PAGES * PAGE_SIZE
DEC_SEQ = SUBLANES


def _top_rank_lanes(vals, n):
    lane = lax.broadcasted_iota(jnp.int32, vals.shape, 1)
    rank = jnp.zeros(vals.shape, F32)
    for j in range(n):
        vj = vals[:, j:j + 1]
        beats = (vj > vals) | ((vj == vals) & (j < lane))
        rank = rank + beats.astype(F32)
    return rank


def _stack_heads(q, lane_heads):
    return jnp.concatenate([_head_mask(q, h) for h in lane_heads], axis=0)


def _pad_rows(x, rows=LANES):
    return jnp.concatenate([x, jnp.zeros((rows - x.shape[0], x.shape[1]), x.dtype)], axis=0)


def _softmax_pv(groups):
    ms = []
    for parts in groups:
        m = parts[0][0].max(axis=1, keepdims=True)
        for s, _, _ in parts[1:]:
            m = jnp.maximum(m, s.max(axis=1, keepdims=True))
        ms.append(m)
    ps, ls = [], []
    for parts, m in zip(groups, ms):
        p = [jnp.exp(s - m) for s, _, _ in parts]
        ls.append(sum(jnp.sum(x, axis=1, keepdims=True) for x in p))
        ps.append([x.astype(BF16) for x in p])
    outs = []
    for parts, p, l in zip(groups, ps, ls):
        o = 0.0
        for (_, v, feature_major), pb in zip(parts, p):
            if feature_major:
                o = o + lax.dot_general(pb, v, _NT, preferred_element_type=F32)
            else:
                o = o + jnp.dot(pb, v, preferred_element_type=F32)
        outs.append(o / l)
    return outs


def _lane_cat(pages, rows):
    return jnp.concatenate([pg[0, rows, :] for pg in pages], axis=1)


def _pair_out(o, r0, r1):
    lane = lax.broadcasted_iota(jnp.int32, (DEC_SEQ, LANES), 1)
    return jnp.where(lane < HEAD_DIM, o[r0:r0 + DEC_SEQ], o[r1:r1 + DEC_SEQ])


def _moba_s_kernel(pt_ref, q_ref, new_ref, bp_ref, bn_ref, *rest):
    pages = rest[:N_PAGES]
    o_ref = rest[N_PAGES]
    nblk = PAST_LEN // MOBA_BLOCK
    ppb = MOBA_BLOCK // PAGE_SIZE
    width = A_HEADS * HEAD_DIM
    q = q_ref[0]
    lane = lax.broadcasted_iota(jnp.int32, (2 * DEC_SEQ, LANES), 1)
    t_row = lax.broadcasted_iota(jnp.int32, (2 * DEC_SEQ, LANES), 0) % DEC_SEQ
    lane_sq = lax.broadcasted_iota(jnp.int32, (LANES, LANES), 1)
    groups = []
    for p in range(A_HEADS // 2):
        rows = slice(p * LANES, (p + 1) * LANES)
        cols = slice(p * LANES, (p + 1) * LANES)
        kmean_t = jnp.zeros((LANES, LANES), F32)
        for j in range(nblk):
            both = sum(pg[0, rows, :] for pg in pages[j * ppb:(j + 1) * ppb])
            kmean_t = jnp.where(lane_sq == j, jnp.sum(both, axis=1, keepdims=True) / MOBA_BLOCK, kmean_t)
        q2 = _stack_heads(q[:, cols], (0, 1))
        gate = jnp.dot(q2, kmean_t, precision=HIGHEST, preferred_element_type=F32)
        gm = jnp.where(lane < nblk, gate, NEG_INF)
        chosen = (_top_rank_lanes(gm, nblk) < MOBA_TOPK) & (lane < nblk)
        neg = jnp.where(chosen, 0.0, NEG_INF)
        q2b = q2.astype(BF16)
        ktb = _lane_cat(pages, rows).astype(BF16)
        vtb = _lane_cat(pages, slice(width + p * LANES, width + (p + 1) * LANES)).astype(BF16)
        s_past = jnp.dot(q2b, ktb, preferred_element_type=F32) + bp_ref[p]
        s_past = s_past + jnp.concatenate([jnp.broadcast_to(neg[:, j:j + 1], (2 * DEC_SEQ, MOBA_BLOCK))
                                           for j in range(nblk)], axis=1)
        knp = _pad_rows(new_ref[0, :, cols]).astype(BF16)
        vnp = _pad_rows(new_ref[0, :, width + p * LANES:width + (p + 1) * LANES]).astype(BF16)
        s_new = lax.dot_general(q2b, knp, _NT, preferred_element_type=F32) + bn_ref[p]
        s_new = jnp.where(lane <= t_row, s_new, NEG_INF)
        groups.append([(s_past, vtb, True), (s_new, vnp, False)])
    o_ref[0] = jnp.concatenate([_pair_out(o, 0, DEC_SEQ) for o in _softmax_pv(groups)], axis=1)


def _page_specs(rows):
    def spec(i):
        return pl.BlockSpec((1, rows, PAGE_SIZE), lambda b, pt, *_: (pt[b * N_PAGES + i], 0, 0))
    return [spec(i) for i in range(N_PAGES)]


def _moba_sample(pt, q_s, a_new, cache, bias_past, bias_new):
    nbt = q_s.shape[0]
    row = lambda w: pl.BlockSpec((1, DEC_SEQ, w), lambda b, pt: (b, 0, 0))
    const = lambda *shape: pl.BlockSpec(shape, lambda b, pt: (0,) * len(shape))
    gs = pltpu.PrefetchScalarGridSpec(
        num_scalar_prefetch=1, grid=(nbt,),
        in_specs=[row(Q_COLS), row(AKV_COLS), const(*bias_past.shape), const(*bias_new.shape)]
        + _page_specs(AKV_COLS),
        out_specs=row(A_HEADS * HEAD_DIM))
    return pl.pallas_call(
        _moba_s_kernel, grid_spec=gs,
        out_shape=jax.ShapeDtypeStruct((nbt, DEC_SEQ, A_HEADS * HEAD_DIM), F32),
        compiler_params=_cparams(("arbitrary",)), name="moba_sample",
    )(pt, q_s, a_new, bias_past, bias_new, *([cache] * N_PAGES))


def _split3(x):
    hi = x.astype(BF16)
    r1 = x - hi.astype(F32)
    mid = r1.astype(BF16)
    lo = (r1 - mid.astype(F32)).astype(BF16)
    return hi, mid, lo


def _fox_s_kernel(pt_ref, raw_ref, q_ref, new_ref, small_ref, cm_ref, *rest):
    pages = rest[:N_PAGES]
    lpages = rest[N_PAGES:2 * N_PAGES]
    o_ref = rest[2 * N_PAGES]
    x_s = rest[2 * N_PAGES + 1]
    b = pl.program_id(0)
    width = B_HEADS * HEAD_DIM
    qoff = A_HEADS * HEAD_DIM
    r16 = 2 * DEC_SEQ
    lane = lax.broadcasted_iota(jnp.int32, (r16, LANES), 1)
    t_row = lax.broadcasted_iota(jnp.int32, (r16, LANES), 0) % DEC_SEQ

    x_s[...] = jnp.zeros(x_s.shape, F32)
    for pg in range(N_PAGES):
        r = raw_ref[b * N_PAGES + pg] % SUBLANES
        for h in range(B_HEADS):
            x_s[h * N_PAGES + pg:h * N_PAGES + pg + 1, :] = lpages[pg][h, pl.ds(r, 1), :]
    hi, mid, lo = _split3(x_s[...])
    parts = jnp.dot(jnp.concatenate([hi, mid, lo], axis=0), cm_ref[0], preferred_element_type=F32)
    sfx = parts[0:LANES] + parts[LANES:2 * LANES] + parts[2 * LANES:3 * LANES]
    t_hi, t_mid, t_lo = _split3(sfx[:, LANES:])
    offp = jnp.dot(cm_ref[1, :, 0:LANES], jnp.concatenate([t_hi, t_mid, t_lo], axis=1),
                   preferred_element_type=F32)
    arow = sfx[:, 0:LANES] + (offp[:, 0:LANES] + offp[:, LANES:2 * LANES] + offp[:, 2 * LANES:3 * LANES])

    def decay_row(h):
        return jnp.concatenate([arow[h * N_PAGES + pg:h * N_PAGES + pg + 1, :] for pg in range(N_PAGES)], axis=1)

    lfn = small_ref[0]
    sub = lax.broadcasted_iota(jnp.int32, (DEC_SEQ, LANES), 0)
    lane8 = lax.broadcasted_iota(jnp.int32, (DEC_SEQ, LANES), 1)
    cs = lfn
    for sh in (1, 2, 4):
        cs = cs + jnp.where(sub >= sh, pltpu.roll(cs, sh, 0), 0.0)

    q = q_ref[0]
    groups = []
    for p in range(B_HEADS // 2):
        cols = slice(p * LANES, (p + 1) * LANES)
        q2b = _stack_heads(q[:, qoff + p * LANES:qoff + (p + 1) * LANES], (0, 1)).astype(BF16)
        heads = (2 * p, 2 * p + 1)
        a_rows = jnp.concatenate([jnp.broadcast_to(decay_row(h), (DEC_SEQ, PAST_LEN)) for h in heads], axis=0)
        d_col = jnp.concatenate([cs[:, h:h + 1] for h in heads], axis=0)
        d_row = jnp.concatenate(
            [jnp.broadcast_to(jnp.sum(jnp.where(sub <= lane8, jnp.broadcast_to(lfn[:, h:h + 1], (DEC_SEQ, LANES)),
                                                0.0), axis=0, keepdims=True), (DEC_SEQ, LANES))
             for h in heads], axis=0)
        ktb = _lane_cat(pages, cols).astype(BF16)
        vtb = _lane_cat(pages, slice(width + p * LANES, width + (p + 1) * LANES)).astype(BF16)
        s_past = jnp.dot(q2b, ktb, preferred_element_type=F32) + (a_rows + d_col)
        knp = _pad_rows(new_ref[0, :, cols]).astype(BF16)
        vnp = _pad_rows(new_ref[0, :, width + p * LANES:width + (p + 1) * LANES]).astype(BF16)
        s_new = lax.dot_general(q2b, knp, _NT, preferred_element_type=F32) + (d_col - d_row)
        s_new = jnp.where(lane <= t_row, s_new, NEG_INF)
        groups.append([(s_past, vtb, True), (s_new, vnp, False)])
    o_ref[0] = jnp.concatenate([_pair_out(o, 0, DEC_SEQ) for o in _softmax_pv(groups)], axis=1)


def _fox_consts():
    tok = np.arange(PAGE_SIZE)
    m = np.concatenate([tok[:, None] > tok[None, :], np.ones((PAGE_SIZE, PAGE_SIZE), bool)], axis=1)
    r = np.arange(LANES)
    u = (r[:, None] // N_PAGES == r[None, :] // N_PAGES) & (r[None, :] % N_PAGES > r[:, None] % N_PAGES)
    u = np.concatenate([u, np.zeros((LANES, LANES), bool)], axis=1)
    return jnp.asarray(np.stack([m, u]), BF16)


def _fox_sample(pt, pt_raw, q_s, b_new, small_s, cache, cache_logf, layer):
    nbt = q_s.shape[0]
    row = lambda w: pl.BlockSpec((1, DEC_SEQ, w), lambda b, pt, raw: (b, 0, 0))
    cm = _fox_consts()
    lspecs = [pl.BlockSpec((B_HEADS, SUBLANES, PAGE_SIZE),
                           functools.partial(lambda i, b, pt, raw: (layer, raw[b * N_PAGES + i] // SUBLANES, 0), i))
              for i in range(N_PAGES)]
    gs = pltpu.PrefetchScalarGridSpec(
        num_scalar_prefetch=2, grid=(nbt,),
        in_specs=[row(Q_COLS), row(BKV_COLS), row(SMALL_COLS),
                  pl.BlockSpec(cm.shape, lambda b, pt, raw: (0, 0, 0))] + _page_specs(BKV_COLS) + lspecs,
        out_specs=row(B_HEADS * HEAD_DIM),
        scratch_shapes=[pltpu.VMEM((LANES, PAGE_SIZE), F32)])
    return pl.pallas_call(
        _fox_s_kernel, grid_spec=gs,
        out_shape=jax.ShapeDtypeStruct((nbt, DEC_SEQ, B_HEADS * HEAD_DIM), F32),
        compiler_params=_cparams(("arbitrary",)), name="fox_sample",
    )(pt, pt_raw, q_s, b_new, small_s, cm, *([cache] * N_PAGES), *([cache_logf] * N_PAGES))


def _nsa_s_kernel(pt_ref, q_ref, new_ref, wnew_ref, gate_ref, win_ref, bc_ref, bp_ref, bn_ref, bw_ref,
                  cov_ref, exp_ref, pe_ref, w1_ref, w2_ref, *rest):
    pages = rest[:N_PAGES]
    o_ref = rest[N_PAGES]
    ctok = rest[N_PAGES + 1]
    ncp = PAST_LEN // CMP_STRIDE
    nbs_past = PAST_LEN // SLC_BLOCK
    qoff = (A_HEADS + B_HEADS) * HEAD_DIM
    nrow = C_HEADS * DEC_SEQ
    lane = lax.broadcasted_iota(jnp.int32, (nrow, LANES), 1)
    t_row = lax.broadcasted_iota(jnp.int32, (nrow, LANES), 0) % DEC_SEQ

    q = q_ref[0]
    q2 = jnp.concatenate([_head_mask(q[:, qoff + (s // 2) * LANES:qoff + (s // 2 + 1) * LANES], s % 2)
                          for s in range(C_HEADS)], axis=0)
    q2b = q2.astype(BF16)

    ktb = _lane_cat(pages, slice(2 * LANES, 3 * LANES)).astype(BF16)
    vtb = _lane_cat(pages, slice(3 * LANES, 4 * LANES)).astype(BF16)
    s_past = jnp.dot(q2b, ktb, preferred_element_type=F32) + bp_ref[...]
    knp = _pad_rows(new_ref[0, :, 2 * LANES:3 * LANES]).astype(BF16)
    vnp = _pad_rows(new_ref[0, :, 3 * LANES:4 * LANES]).astype(BF16)
    s_new = lax.dot_general(q2b, knp, _NT, preferred_element_type=F32) + bn_ref[...]
    wkt = win_ref[0, 0:LANES, :].astype(BF16)
    wvt = win_ref[0, LANES:2 * LANES, :].astype(BF16)
    wlane = lax.broadcasted_iota(jnp.int32, (nrow, WINDOW), 1)
    wt = lax.broadcasted_iota(jnp.int32, (nrow, WINDOW), 0) % DEC_SEQ
    s_w = jnp.dot(q2b, wkt, preferred_element_type=F32) + bw_ref[...]
    s_w = jnp.where(wlane > wt, s_w, NEG_INF)
    wkn = _pad_rows(wnew_ref[0, :, 0:LANES]).astype(BF16)
    wvn = _pad_rows(wnew_ref[0, :, LANES:2 * LANES]).astype(BF16)
    s_wn = lax.dot_general(q2b, wkn, _NT, preferred_element_type=F32) + bn_ref[...]
    s_wn = jnp.where(lane <= t_row, s_wn, NEG_INF)

    for i, pg in enumerate(pages):
        for kind in range(2):
            ctok[kind, i * PAGE_SIZE:(i + 1) * PAGE_SIZE, :] = pg[0, kind * LANES:(kind + 1) * LANES, :].T

    def rows_of(kind):
        return lambda l: ctok[kind, pl.ds(l, ncp, stride=CMP_STRIDE), :]

    kc = _compress_pair(rows_of(0), ncp, pe_ref, w1_ref, w2_ref, 0).astype(BF16)
    vc = _compress_pair(rows_of(1), ncp, pe_ref, w1_ref, w2_ref, 1).astype(BF16)

    sc = lax.dot_general(q2b, kc, _NT, preferred_element_type=F32) + bc_ref[...]
    sc = jnp.where(lane < ncp - 1, sc, NEG_INF)
    m = sc.max(axis=1, keepdims=True)
    e = jnp.exp(sc - m)
    pc = e / jnp.sum(e, axis=1, keepdims=True)
    o_cmp = jnp.dot(pc.astype(BF16), vc, preferred_element_type=F32)

    psum = [sum(pc[s * DEC_SEQ:(s + 1) * DEC_SEQ] for s in range(kvh, C_HEADS, 2)) for kvh in range(2)]
    imp = jnp.dot(jnp.concatenate(psum, axis=0), cov_ref[...], precision=HIGHEST, preferred_element_type=F32)
    lane16 = lax.broadcasted_iota(jnp.int32, (2 * DEC_SEQ, LANES), 1)
    forced = (lane16 == 0) | (lane16 == nbs_past) | (lane16 == nbs_past - 1)
    v = jnp.where(lane16 <= nbs_past, jnp.where(forced, FORCE_SCORE, imp), NEG_INF)
    chosen = (_top_rank_lanes(v, nbs_past + 1) < SLC_TOPK) & (lane16 <= nbs_past)
    chosen_f = jnp.where(chosen, 1.0, 0.0)
    mask16 = jnp.dot(chosen_f.astype(BF16), exp_ref[...], preferred_element_type=F32)
    mask_past = jnp.concatenate([mask16[(s % 2) * DEC_SEQ:(s % 2 + 1) * DEC_SEQ] for s in range(C_HEADS)], axis=0)
    mask_new = jnp.concatenate(
        [jnp.broadcast_to(chosen_f[(s % 2) * DEC_SEQ:(s % 2 + 1) * DEC_SEQ, nbs_past:nbs_past + 1],
                          (DEC_SEQ, LANES)) for s in range(C_HEADS)], axis=0)

    s_past = jnp.where(mask_past > 0.5, s_past, NEG_INF)
    s_new = jnp.where((lane <= t_row) & (mask_new > 0.5), s_new, NEG_INF)
    o_slc, o_win = _softmax_pv([[(s_past, vtb, True), (s_new, vnp, False)],
                                [(s_w, wvt, True), (s_wn, wvn, False)]])

    g = jax.nn.sigmoid(gate_ref[0])
    o = g[:, 0:1] * o_cmp + g[:, 1:2] * o_slc + g[:, 2:3] * o_win
    o_ref[0] = jnp.concatenate([_pair_out(o, 2 * m_ * DEC_SEQ, (2 * m_ + 1) * DEC_SEQ)
                                for m_ in range(C_HEADS // 2)], axis=1)


def _nsa_sample(pt, q_s, c_new, w_new, gates, win_state, layer, cache, tables, pe2, w1bd, w2bd):
    nbt = q_s.shape[0]
    bc, bp, bn, bw, cov, expand = tables
    row = lambda r, w: pl.BlockSpec((1, r, w), lambda b, pt: (b, 0, 0))
    win_spec = pl.BlockSpec((1, CWIN_COLS, WINDOW), lambda b, pt: (layer * nbt + b, 0, 0))
    const = lambda a: pl.BlockSpec(a.shape, lambda b, pt: (0,) * a.ndim)
    gs = pltpu.PrefetchScalarGridSpec(
        num_scalar_prefetch=1, grid=(nbt,),
        in_specs=[row(DEC_SEQ, Q_COLS), row(DEC_SEQ, CKV_COLS), row(DEC_SEQ, CWIN_COLS),
                  row(C_HEADS * DEC_SEQ, LANES), win_spec,
                  const(bc), const(bp), const(bn), const(bw), const(cov), const(expand),
                  const(pe2), const(w1bd), const(w2bd)] + _page_specs(CKV_COLS),
        out_specs=row(DEC_SEQ, C_HEADS * HEAD_DIM),
        scratch_shapes=[pltpu.VMEM((2, PAST_LEN, LANES), F32)])
    return pl.pallas_call(
        _nsa_s_kernel, grid_spec=gs,
        out_shape=jax.ShapeDtypeStruct((nbt, DEC_SEQ, C_HEADS * HEAD_DIM), F32),
        compiler_params=_cparams(("arbitrary",)), name="nsa_sample",
    )(pt, q_s, c_new, w_new, gates, win_state, bc, bp, bn, bw, cov, expand, pe2, w1bd, w2bd,
      *([cache] * N_PAGES))


def _bucket_of(dist):
    return _BUCKETS[np.clip(dist, 0, MAX_DISTANCE)].astype(np.int32)


_SAMPLE_KEY_POS = np.concatenate([
    np.arange(PAST_LEN),
    PAST_LEN + np.arange(LANES),
    PAST_LEN - WINDOW + np.arange(WINDOW),
    np.arange(PAST_LEN // CMP_STRIDE) * CMP_STRIDE + CMP_BLOCK - 1
])


def _bias_bucket_tables(t):
    k = np.arange(TQ)[:, None]
    q = np.arange(TQ)[None, :]
    ncp = t // CMP_STRIDE
    rel = np.arange(2 * ncp)[:, None] - ncp
    band = _bucket_of(q - (rel * CMP_STRIDE + CMP_BLOCK - 1))
    samp = _bucket_of(PAST_LEN + np.arange(DEC_SEQ)[:, None] - _SAMPLE_KEY_POS[None, :])
    masked = np.full((TQ, TQ), _MASKED_CODE, np.int32)
    own = np.where(q >= k, _bucket_of(q - k), masked)
    prev = _bucket_of(TQ + q - k)
    far = _bucket_of(np.full((TQ, TQ), 2 * TQ))
    far_win = np.where(k > q, far, masked)
    tiles3 = np.concatenate([own, prev, far], axis=0)
    tiles5 = np.concatenate([far_win, prev, own] + [masked] * (WINDOW // TQ), axis=0)
    return [jnp.asarray(a) for a in (tiles3, tiles5, band, samp)]


_MASKED_CODE = N_BUCKETS


def _bias_kernel(tab_ref, *refs):
    h = pl.program_id(0)
    n = len(refs) // 2
    for idx_ref, o_ref in zip(refs[:n], refs[n:]):
        idx = idx_ref[...]
        acc = jnp.full(idx.shape, NEG_INF, F32)
        for bkt in range(N_BUCKETS):
            acc = jnp.where(idx == bkt, tab_ref[bkt, h], acc)
        o_ref[0] = acc


def _bias_tables(rel_bias, t):
    idx = _bias_bucket_tables(t)
    nh = rel_bias.shape[1]
    return pl.pallas_call(
        _bias_kernel,
        grid=(nh,),
        in_specs=[pl.BlockSpec(memory_space=pltpu.SMEM)]
        + [pl.BlockSpec(a.shape, lambda h: (0, 0)) for a in idx],
        out_specs=[pl.BlockSpec((1,) + a.shape, lambda h: (h, 0, 0)) for a in idx],
        out_shape=[jax.ShapeDtypeStruct((nh,) + a.shape, F32) for a in idx],
        compiler_params=_cparams(("arbitrary",)),
        name="bias_tables",
    )(rel_bias, *idx)


def _sample_masks():
    ncp = PAST_LEN // CMP_STRIDE
    nbs = PAST_LEN // SLC_BLOCK
    ci = np.arange(ncp)[:, None]
    bj = np.arange(LANES)[None, :]
    cov = ((ci * CMP_STRIDE < bj * SLC_BLOCK + SLC_BLOCK) & (ci * CMP_STRIDE + CMP_BLOCK > bj * SLC_BLOCK)
           & (ci < ncp - 1) & (bj < nbs))
    expand = np.arange(LANES)[:, None] == (np.arange(PAST_LEN)[None, :] // SLC_BLOCK)
    return jnp.asarray(cov, F32), jnp.asarray(expand, BF16)


def _prep_compress(cmp_pe, cmp_w1, cmp_w2):
    pe2 = jnp.concatenate([cmp_pe, cmp_pe], axis=-1)
    z1 = jnp.zeros_like(cmp_w1)
    w1bd = jnp.concatenate([jnp.concatenate([cmp_w1, z1], axis=-1),
                            jnp.concatenate([z1, cmp_w1], axis=-1)], axis=-2)
    z2 = jnp.zeros_like(cmp_w2)
    w2bd = jnp.concatenate([jnp.concatenate([cmp_w2, z2], axis=-1),
                            jnp.concatenate([z2, cmp_w2], axis=-1)], axis=-2)
    return pe2, w1bd.astype(BF16), w2bd.astype(BF16)


def _cover_t(nbs, ncp):
    cstart = np.arange(ncp)[None, :] * CMP_STRIDE
    bstart = np.arange(nbs)[:, None] * SLC_BLOCK
    cov = (cstart < bstart + SLC_BLOCK) & (cstart + CMP_BLOCK > bstart) & (np.arange(ncp)[None, :] < ncp - 1)
    return jnp.asarray(cov, F32)


_SPLIT_SIZES = (256, 256, 256, 384, 384, 384, 6, 384, 128, 128, 128, 128, 128, 128, 18)


def _prep_w_in(w_in):
    offs = np.cumsum(_SPLIT_SIZES)[:-1].tolist()
    aq, ak, av, bq, bk, bv, bf, cq, cck, ccv, csk, csv, cwk, cwv, cg = jnp.split(w_in, offs, axis=-1)
    lead = cq.shape[:-1]
    cq = jnp.concatenate([cq[..., h * HEAD_DIM:(h + 1) * HEAD_DIM] for h in C_SLOT_HEADS], axis=-1)
    scale = HEAD_DIM ** -0.5
    pad = jnp.zeros(lead + (SMALL_COLS - B_HEADS - C_HEADS * N_GATES,), w_in.dtype)
    w = jnp.concatenate([aq * scale, bq * scale, cq * scale, ak, av, bk, bv,
                         cck, ccv, csk, csv, cwk, cwv, bf, cg, pad], axis=-1)
    return w.astype(BF16)


def _prep_small_bias(b_f, b_gate):
    pad = jnp.zeros(b_f.shape[:-1] + (SMALL_COLS - B_HEADS - C_HEADS * N_GATES,), F32)
    return jnp.concatenate([b_f, b_gate, pad], axis=-1)[:, None, :]


def _prep_w_out(w_out):
    wa = w_out[:, :A_HEADS * HEAD_DIM]
    wb = w_out[:, A_HEADS * HEAD_DIM:(A_HEADS + B_HEADS) * HEAD_DIM]
    wc = w_out[:, (A_HEADS + B_HEADS) * HEAD_DIM:]
    wc = jnp.concatenate([wc[:, h * HEAD_DIM:(h + 1) * HEAD_DIM] for h in C_SLOT_HEADS], axis=1)
    return wa.astype(BF16), wb.astype(BF16), wc.astype(BF16)


def kernel(x_prompt, x_sample, cache_a_kv, cache_b_kv, cache_b_logf, cache_c_kv, state_c_win, state_ffn_conv,
           page_table, rel_bias, ln1_g, w_in, b_f, b_gate, cmp_pe, cmp_w1, cmp_w2, w_out, ln2_g, w_gate, w_up,
           conv_w, conv_b, w_down, final_g):
    depth = w_in.shape[0]
    b, t, _ = x_prompt.shape
    nbt, ts, _ = x_sample.shape
    n_pool = cache_a_kv.shape[1]
    assert ts == DEC_SEQ and page_table.shape[1] == N_PAGES and cache_a_kv.shape[2] == PAGE_SIZE
    assert t % TQ == 0 and state_c_win.shape[2] == WINDOW and t >= WINDOW
    ffn_tm = 1024

    w_in_p = _prep_w_in(w_in)
    small_bias = _prep_small_bias(b_f, b_gate)
    pe2, w1bd, w2bd = _prep_compress(cmp_pe, cmp_w1, cmp_w2)
    woa, wob, woc = _prep_w_out(w_out)
    wg, wu, wd = w_gate.astype(BF16), w_up.astype(BF16), w_down.astype(BF16)

    tiles3, tiles5, band, samp = _bias_tables(rel_bias, t)
    c_slots = lambda x: jnp.stack([x[A_HEADS + h] for h in C_SLOT_HEADS], axis=0)
    tiles3_a, tiles3_c, tiles5_c, band_c = tiles3[:A_HEADS], c_slots(tiles3), c_slots(tiles5), c_slots(band)
    cover_t = _cover_t(t // SLC_BLOCK, t // CMP_STRIDE)
    o1, o2, o3 = PAST_LEN, PAST_LEN + LANES, PAST_LEN + LANES + WINDOW
    samp_a = samp[:A_HEADS].reshape(A_HEADS // 2, 2 * DEC_SEQ, -1)
    moba_tabs = (samp_a[..., :o1], samp_a[..., o1:o2])
    samp_c = c_slots(samp).reshape(C_HEADS * DEC_SEQ, -1)
    nsa_tabs = (samp_c[:, o3:], samp_c[:, :o1], samp_c[:, o1:o2], samp_c[:, o2:o3]) + _sample_masks()

    fm = lambda c: jnp.transpose(c, (0, 1, 3, 4, 5, 2))
    cache_a = fm(cache_a_kv).reshape(depth * n_pool, AKV_COLS, PAGE_SIZE)
    cache_b = fm(cache_b_kv).reshape(depth * n_pool, BKV_COLS, PAGE_SIZE)
    cache_c = fm(cache_c_kv).reshape(depth * n_pool, CKV_COLS, PAGE_SIZE)
    cache_lf = jnp.transpose(cache_b_logf, (0, 3, 1, 2)).reshape(depth * B_HEADS, n_pool, PAGE_SIZE)
    win_all = fm(state_c_win).reshape(depth * nbt, CWIN_COLS, WINDOW)
    pt_flat = page_table.reshape(-1).astype(jnp.int32)

    xp = x_prompt.reshape(b * t, D_MODEL)
    xs = x_sample.reshape(nbt * ts, D_MODEL)
    outs_p = [[] for _ in range(6)]
    outs_s = [[] for _ in range(6)]
    for l in range(depth):
        final = l == depth - 1
        q, akv, bkv, ckv, cwin, small = [o.reshape(b, t, -1) for o in
                                         _in_proj(xp, ln1_g[l], w_in_p[l], small_bias[l])]
        ya = _moba_prompt(q, akv, tiles3_a)
        yb = _fox_prompt(q, bkv, small)
        kc, vc = _compress_prompt(ckv, pe2[l], w1bd[l], w2bd[l])
        yc = _nsa_prompt(q, ckv, cwin, jnp.swapaxes(small, 1, 2), tiles3_c, tiles5_c, band_c, cover_t, kc, vc)
        xp, gk = _ffn(xp, ya.reshape(b * t, -1), yb.reshape(b * t, -1), yc.reshape(b * t, -1),
                      woa[l], wob[l], woc[l], ln2_g[l], final_g, wg[l], wu[l], wd[l], conv_w[l], conv_b[l],
                      sample=False, blocks_per_seq=t // ffn_tm, final=final, tm=ffn_tm)
        outs_p[0].append(akv.reshape(b, t, 2, A_HEADS, HEAD_DIM))
        outs_p[1].append(bkv.reshape(b, t, 2, B_HEADS, HEAD_DIM))
        outs_p[2].append(small[..., :B_HEADS])
        outs_p[3].append(ckv.reshape(b, t, 4, C_KV_HEADS, HEAD_DIM))
        outs_p[4].append(cwin[:, t - WINDOW:].reshape(b, WINDOW, 2, C_KV_HEADS, HEAD_DIM))
        outs_p[5].append(gk.reshape(b, t // ffn_tm, SUBLANES, D_FF)[:, -1, SUBLANES - (CONV_W - 1):])
        q, akv, bkv, ckv, cwin, small = [o.reshape(nbt, ts, -1) for o in
                                         _in_proj(xs, ln1_g[l], w_in_p[l], small_bias[l])]
        pt = pt_flat + l * n_pool
        ya = _moba_sample(pt, q, akv, cache_a, *moba_tabs)
        yb = _fox_sample(pt, pt_flat, q, bkv, small, cache_b, cache_lf, l)
        gl = small[..., B_HEADS:B_HEADS + C_HEADS * N_GATES].reshape(nbt, ts, C_HEADS, N_GATES)
        gl = jnp.concatenate([gl[:, :, h] for h in C_SLOT_HEADS], axis=1)
        gl = jnp.pad(gl, ((0, 0), (0, 0), (0, LANES - N_GATES)))
        yc = _nsa_sample(pt, q, ckv, cwin, gl, win_all, l, cache_c, nsa_tabs, pe2[l], w1bd[l], w2bd[l])
        prev = state_ffn_conv[l]
        zero = jnp.zeros((nbt, ts - 2, D_FF), F32)
        p1 = jnp.concatenate([prev[:, 1:2], zero, zero[:, :1]], axis=1).reshape(nbt * ts, D_FF)
        p2 = jnp.concatenate([prev, zero], axis=1).reshape(nbt * ts, D_FF)
        xs, g_s = _ffn(xs, ya.reshape(nbt * ts, -1), yb.reshape(nbt * ts, -1), yc.reshape(nbt * ts, -1),
                       woa[l], wob[l], woc[l], ln2_g[l], final_g, wg[l], wu[l], wd[l], conv_w[l], conv_b[l],
                       sample=True, final=final, inj=(p1, p2), tm=ffn_tm)
        outs_s[0].append(akv.reshape(nbt, ts, 2, A_HEADS, HEAD_DIM))
        outs_s[1].append(bkv.reshape(nbt, ts, 2, B_HEADS, HEAD_DIM))
        outs_s[2].append(small[..., :B_HEADS])
        outs_s[3].append(ckv.reshape(nbt, ts, 4, C_KV_HEADS, HEAD_DIM))
        outs_s[4].append(jnp.concatenate([state_c_win[l][:, ts:],
                                          cwin.reshape(nbt, ts, 2, C_KV_HEADS, HEAD_DIM)], axis=1))
        outs_s[5].append(g_s.reshape(nbt, ts, D_FF)[:, ts - (CONV_W - 1):])
    st = lambda rows: jnp.stack(rows, axis=0)
    return (xp.reshape(b, t, D_MODEL), xs.reshape(nbt, ts, D_MODEL),
            *[st(r) for r in outs_p], *[st(r) for r in outs_s])
```

```python
import functools
import math

import numpy as np
import jax
import jax.numpy as jnp
from jax import lax
from jax.experimental import pallas as pl
from jax.experimental.pallas import tpu as pltpu

F32 = jnp.float32
BF16 = jnp.bfloat16
HIGHEST = lax.Precision.HIGHEST

D_MODEL = 1024
HEAD_DIM = 64
A_HEADS = 4
B_HEADS = 6
C_HEADS = 6
C_KV_HEADS = 2
C_GROUP = 3
MOBA_BLOCK = 256
MOBA_TOPK = 3
CMP_BLOCK = 32
CMP_STRIDE = 16
CMP_HIDDEN = 128
SLC_BLOCK = 64
SLC_TOPK = 8
WINDOW = 512
N_GATES = 3
N_BUCKETS = 32
MAX_DISTANCE = 128
D_FF = 2816
CONV_W = 3
RMS_EPS = 1e-6
NEG_INF = -1e30
FORCE_SCORE = 1e6
PAGE_SIZE = 128

LANES = 128
SUBLANES = 8
TQ = 256
VMEM_LIMIT = 56 * 1024 * 1024

C_SLOT_HEADS = (0, 3, 1, 4, 2, 5)

Q_COLS = (A_HEADS + B_HEADS + C_HEADS) * HEAD_DIM
AKV_COLS = 2 * A_HEADS * HEAD_DIM
BKV_COLS = 2 * B_HEADS * HEAD_DIM
CKV_COLS = 4 * C_KV_HEADS * HEAD_DIM
CWIN_COLS = 2 * C_KV_HEADS * HEAD_DIM
SMALL_COLS = LANES
IN_COLS = Q_COLS + AKV_COLS + BKV_COLS + CKV_COLS + CWIN_COLS + SMALL_COLS

_NT = (((1,), (1,)), ((), ()))


def _bucket_table():
    n = np.arange(MAX_DISTANCE + 1)
    exact = N_BUCKETS // 2
    nf = np.maximum(n, 1).astype(np.float32)
    large = exact + (np.log(nf / exact) / math.log(MAX_DISTANCE / exact) * (N_BUCKETS - exact)).astype(np.int32)
    large = np.minimum(large, N_BUCKETS - 1)
    return np.where(n < exact, n, large).astype(np.int32)


_BUCKETS = _bucket_table()


def _cparams(sem, vmem=VMEM_LIMIT):
    return pltpu.CompilerParams(dimension_semantics=sem, vmem_limit_bytes=vmem)


def _gelu(x):
    return 0.5 * x * (1.0 + jnp.tanh(math.sqrt(2.0 / math.pi) * (x + 0.044715 * (x * x * x))))


def _log_sigmoid(x):
    return -(jnp.maximum(-x, 0.0) + jnp.log1p(jnp.exp(-jnp.abs(x))))


def _in_proj_kernel(x_ref, g_ref, w_ref, sb_ref, q_ref, akv_ref, bkv_ref, ckv_ref, cwin_ref, small_ref):
    x = x_ref[...]
    ms = jnp.mean(x * x, axis=-1, keepdims=True)
    h = (x * lax.rsqrt(ms + RMS_EPS)) * g_ref[...]
    hb = h.astype(BF16)
    off = 0
    for ref in (q_ref, akv_ref, bkv_ref, ckv_ref, cwin_ref):
        n = ref.shape[-1]
        ref[...] = jnp.dot(hb, w_ref[:, off:off + n], preferred_element_type=F32)
        off += n
    raw = jnp.dot(hb, w_ref[:, off:off + SMALL_COLS], preferred_element_type=F32) + sb_ref[...]
    lane = lax.broadcasted_iota(jnp.int32, raw.shape, 1)
    small_ref[...] = jnp.where(lane < B_HEADS, _log_sigmoid(raw), raw)


def _in_proj(x2d, g, w, small_bias, tm=256):
    n = x2d.shape[0]
    tm = min(tm, n)
    widths = (Q_COLS, AKV_COLS, BKV_COLS, CKV_COLS, CWIN_COLS, SMALL_COLS)
    return pl.pallas_call(
        _in_proj_kernel,
        grid=(n // tm,),
        in_specs=[pl.BlockSpec((tm, D_MODEL), lambda i: (i, 0)),
                  pl.BlockSpec((1, D_MODEL), lambda i: (0, 0)),
                  pl.BlockSpec((D_MODEL, IN_COLS), lambda i: (0, 0)),
                  pl.BlockSpec((1, SMALL_COLS), lambda i: (0, 0))],
        out_specs=[pl.BlockSpec((tm, c), lambda i: (i, 0)) for c in widths],
        out_shape=[jax.ShapeDtypeStruct((n, c), F32) for c in widths],
        compiler_params=_cparams(("parallel",)),
        name="in_proj",
    )(x2d, g.reshape(1, D_MODEL), w, small_bias)


def _flash_init(s, vt):
    m = jnp.max(s, axis=0, keepdims=True)
    p = jnp.exp(s - m)
    l = jnp.sum(p, axis=0, keepdims=True)
    acc = jnp.dot(vt, p.astype(BF16), preferred_element_type=F32)
    return m, l, acc


def _attend_t(scs, vts):
    ones = jnp.ones((2 * SUBLANES, scs[0].shape[0]), BF16)
    ms = [jnp.max(s, axis=0, keepdims=True) for s in scs]
    ps = [jnp.exp(s - m).astype(BF16) for s, m in zip(scs, ms)]
    accs = [jnp.dot(jnp.concatenate([vt, ones], axis=0), p, preferred_element_type=F32) for vt, p in zip(vts, ps)]
    return [a[0:HEAD_DIM] / a[HEAD_DIM:HEAD_DIM + 1] for a in accs]


def _flash_step(s, vt, m, l, acc):
    m_new = jnp.maximum(m, jnp.max(s, axis=0, keepdims=True))
    a = jnp.exp(m - m_new)
    p = jnp.exp(s - m_new)
    l = a * l + jnp.sum(p, axis=0, keepdims=True)
    acc = a * acc + jnp.dot(vt, p.astype(BF16), preferred_element_type=F32)
    return m_new, l, acc


def _flash_pipeline(n_tiles, score_fn, vt_fn, sbuf, heads=2, first=None):
    for h in range(heads):
        sbuf[h] = score_fn(h, 0) if first is None else first(h)
    pad = 2 * SUBLANES
    ones = jnp.ones((pad, TQ), BF16)
    init = []
    for h in range(heads):
        init += [jnp.full((1, TQ), NEG_INF, F32), jnp.zeros((HEAD_DIM + pad, TQ), F32)]

    def body(i, st):
        cur = [sbuf[h] for h in range(heads)]
        nxt = [score_fn(h, jnp.minimum(i + 1, n_tiles - 1)) for h in range(heads)]
        out = []
        for h in range(heads):
            m, acc = st[2 * h:2 * h + 2]
            m_new = jnp.maximum(m, jnp.max(cur[h], axis=0, keepdims=True))
            p = jnp.exp(cur[h] - m_new).astype(BF16)
            vt1 = jnp.concatenate([vt_fn(h, i), ones], axis=0)
            acc = jnp.exp(m - m_new) * acc + jnp.dot(vt1, p, preferred_element_type=F32)
            out.extend((m_new, acc))
        for h in range(heads):
            sbuf[h] = nxt[h]
        return tuple(out)

    st = lax.fori_loop(0, n_tiles, body, tuple(init))
    out = []
    for h in range(heads):
        acc = st[2 * h + 1]
        out.extend((st[2 * h], acc[HEAD_DIM:HEAD_DIM + 1, :], acc[0:HEAD_DIM, :]))
    return out


def _top_rank(vals, row):
    rank = jnp.zeros(vals.shape, F32)
    for j in range(vals.shape[0]):
        vj = vals[j:j + 1, :]
        beats = (vj > vals) | ((vj == vals) & (j < row))
        rank = rank + beats.astype(F32)
    return rank


def _head_mask(q, h):
    lane = lax.broadcasted_iota(jnp.int32, (1, LANES), 1)
    return jnp.where((lane // HEAD_DIM) == h, q, 0.0)


def _moba_p_kernel(q0_ref, q1_ref, k_ref, v_ref, b3_ref, o_ref, kb, vtb, kmean, sel, sbuf):
    qi = pl.program_id(1)
    nb = kmean.shape[0]

    @pl.when(qi == 0)
    def _prep():
        k = k_ref[0]
        kb[...] = k.astype(BF16)
        vtb[...] = v_ref[0].T.astype(BF16)
        for j in range(nb):
            kmean[j:j + 1, :] = jnp.mean(k[j * TQ:(j + 1) * TQ, :], axis=0, keepdims=True)

    rowj = lax.broadcasted_iota(jnp.int32, (nb, TQ), 0)
    qbs = []
    for h in range(A_HEADS):
        pair = slice((h // 2) * LANES, (h // 2 + 1) * LANES)
        qh = _head_mask((q0_ref, q1_ref)[h // 2][0], h % 2)
        qbs.append(qh.astype(BF16))
        g = lax.dot_general(kmean[:, pair], qh, _NT, precision=HIGHEST, preferred_element_type=F32)
        gm = jnp.where(rowj < qi, g, NEG_INF)
        chosen = ((_top_rank(gm, rowj) < MOBA_TOPK) & (rowj < qi)) | (rowj == qi)
        sel[h] = jnp.where(chosen, 0.0, NEG_INF)

    def score_fn(h, i):
        j = qi - i
        kblk = kb[pl.ds(pl.multiple_of(j * TQ, TQ), TQ), (h // 2) * LANES:(h // 2 + 1) * LANES]
        bias = b3_ref[h, pl.ds(pl.multiple_of(jnp.minimum(i, 2) * TQ, TQ), TQ), :]
        return lax.dot_general(kblk, qbs[h], _NT, preferred_element_type=F32) + bias + sel[h, pl.ds(j, 1), :]

    def vt_fn(h, i):
        return vtb[h * HEAD_DIM:(h + 1) * HEAD_DIM, pl.ds(pl.multiple_of((qi - i) * TQ, TQ), TQ)]

    st = _flash_pipeline(qi + 1, score_fn, vt_fn, sbuf, heads=A_HEADS)
    o = jnp.concatenate([st[3 * h + 2] / st[3 * h + 1] for h in range(A_HEADS)], axis=0)
    o_ref[0] = o.T


def _moba_prompt(q_all, a_kv, tiles3):
    b, t, _ = q_all.shape
    width = A_HEADS * HEAD_DIM
    nb = t // TQ
    return pl.pallas_call(
        _moba_p_kernel,
        grid=(b, nb),
        in_specs=[pl.BlockSpec((1, TQ, LANES), lambda bi, qi: (bi, qi, 0)),
                  pl.BlockSpec((1, TQ, LANES), lambda bi, qi: (bi, qi, 1)),
                  pl.BlockSpec((1, t, width), lambda bi, qi: (bi, 0, 0)),
                  pl.BlockSpec((1, t, width), lambda bi, qi: (bi, 0, 1)),
                  pl.BlockSpec((A_HEADS, 3 * TQ, TQ), lambda bi, qi: (0, 0, 0))],
        out_specs=pl.BlockSpec((1, TQ, width), lambda bi, qi: (bi, qi, 0)),
        out_shape=jax.ShapeDtypeStruct((b, t, width), F32),
        scratch_shapes=[pltpu.VMEM((t, width), BF16), pltpu.VMEM((width, t), BF16),
                        pltpu.VMEM((nb, width), F32), pltpu.VMEM((A_HEADS, nb, TQ), F32),
                        pltpu.VMEM((A_HEADS, TQ, TQ), F32)],
        compiler_params=_cparams(("parallel", "arbitrary")),
        name="moba_prompt",
    )(q_all, q_all, a_kv, a_kv, tiles3)


def _fox_p_kernel(q0_ref, q1_ref, q2_ref, k_ref, v_ref, lf_ref, o_ref, kb, vtb, ckb, sbuf):
    qi = pl.program_id(1)
    t = kb.shape[0]
    nb = t // TQ

    @pl.when(qi == 0)
    def _prep():
        kb[...] = k_ref[0].astype(BF16)
        vtb[...] = v_ref[0].T.astype(BF16)
        r = lax.broadcasted_iota(jnp.int32, (TQ, TQ), 0)
        c = lax.broadcasted_iota(jnp.int32, (TQ, TQ), 1)
        tri = (c <= r).astype(BF16)
        sr = lax.broadcasted_iota(jnp.int32, (LANES, LANES), 0)
        parts = _split3(lf_ref[0])
        for h in range(B_HEADS):
            pick = (sr == h).astype(BF16)
            sel3 = jnp.concatenate([jnp.dot(x, pick, preferred_element_type=F32).astype(BF16) for x in parts],
                                   axis=1)
            carry = jnp.zeros((1, LANES), F32)
            for j in range(nb):
                c3 = jnp.dot(tri, sel3[j * TQ:(j + 1) * TQ, :], preferred_element_type=F32)
                blk = c3[:, 0:LANES] + c3[:, LANES:2 * LANES] + c3[:, 2 * LANES:3 * LANES] + carry
                ckb[h, j * TQ:(j + 1) * TQ, :] = blk
                carry = blk[TQ - 1:TQ, :]

    q_refs = (q0_ref, q1_ref, q2_ref)
    qbs = [_head_mask(q_refs[h // 2][0], h % 2).astype(BF16) for h in range(B_HEADS)]
    kk = lax.broadcasted_iota(jnp.int32, (TQ, LANES), 0)
    for h in range(B_HEADS):
        own = ckb[h, pl.ds(pl.multiple_of(qi * TQ, TQ), TQ), :]
        for half in range(TQ // LANES):
            qq = lax.broadcasted_iota(jnp.int32, (TQ, LANES), 1) + half * LANES
            ckb[h, t + half * TQ:t + (half + 1) * TQ, :] = jnp.where(kk <= qq, own, -NEG_INF)

    def k_tile(h, j):
        return kb[pl.ds(pl.multiple_of(j * TQ, TQ), TQ), (h // 2) * LANES:(h // 2 + 1) * LANES]

    def score_fn(h, i):
        j = qi - i
        s = lax.dot_general(k_tile(h, j), qbs[h], _NT, preferred_element_type=F32)
        c = ckb[h, pl.ds(pl.multiple_of(j * TQ, TQ), TQ), :]
        return s - jnp.concatenate([c, c], axis=1)

    def score_own(h):
        s = lax.dot_general(k_tile(h, qi), qbs[h], _NT, preferred_element_type=F32)
        return s - jnp.concatenate([ckb[h, t:t + TQ, :], ckb[h, t + TQ:t + 2 * TQ, :]], axis=1)

    def vt_fn(h, i):
        return vtb[h * HEAD_DIM:(h + 1) * HEAD_DIM, pl.ds(pl.multiple_of((qi - i) * TQ, TQ), TQ)]

    st = _flash_pipeline(qi + 1, score_fn, vt_fn, sbuf, heads=B_HEADS, first=score_own)
    o = jnp.concatenate([st[3 * h + 2] / st[3 * h + 1] for h in range(B_HEADS)], axis=0)
    o_ref[0] = o.T


def _fox_prompt(q_all, b_kv, small):
    b, t, _ = q_all.shape
    width = B_HEADS * HEAD_DIM
    qoff = A_HEADS // 2
    nb = t // TQ
    qspec = lambda m: pl.BlockSpec((1, TQ, LANES), lambda bi, qi: (bi, qi, qoff + m))
    return pl.pallas_call(
        _fox_p_kernel,
        grid=(b, nb),
        in_specs=[qspec(0), qspec(1), qspec(2),
                  pl.BlockSpec((1, t, width), lambda bi, qi: (bi, 0, 0)),
                  pl.BlockSpec((1, t, width), lambda bi, qi: (bi, 0, 1)),
                  pl.BlockSpec((1, t, LANES), lambda bi, qi: (bi, 0, 0))],
        out_specs=pl.BlockSpec((1, TQ, width), lambda bi, qi: (bi, qi, 0)),
        out_shape=jax.ShapeDtypeStruct((b, t, width), F32),
        scratch_shapes=[pltpu.VMEM((t, width), BF16), pltpu.VMEM((width, t), BF16),
                        pltpu.VMEM((B_HEADS, t + 2 * TQ, LANES), F32), pltpu.VMEM((B_HEADS, TQ, TQ), F32)],
        compiler_params=_cparams(("parallel", "arbitrary")),
        name="fox_prompt",
    )(q_all, q_all, q_all, b_kv, b_kv, small)


def _compress_pair(rows_of, ncp, pe_ref, w1_ref, w2_ref, kind):
    half = CMP_BLOCK // 2
    acc_a = jnp.zeros((ncp, 2 * CMP_HIDDEN), F32)
    acc_b = jnp.zeros((ncp, 2 * CMP_HIDDEN), F32)
    for l in range(0, half, 2):
        r0, r1 = rows_of(l), rows_of(l + 1)
        xa = jnp.concatenate([(r0 + pe_ref[kind, l:l + 1, :]).astype(BF16),
                              (r1 + pe_ref[kind, l + 1:l + 2, :]).astype(BF16)], axis=1)
        xb = jnp.concatenate([(r0 + pe_ref[kind, half + l:half + l + 1, :]).astype(BF16),
                              (r1 + pe_ref[kind, half + l + 1:half + l + 2, :]).astype(BF16)], axis=1)
        wa = jnp.concatenate([w1_ref[kind, l], w1_ref[kind, l + 1]], axis=0)
        wb = jnp.concatenate([w1_ref[kind, half + l], w1_ref[kind, half + l + 1]], axis=0)
        acc_a = acc_a + jnp.dot(xa, wa, preferred_element_type=F32)
        acc_b = acc_b + jnp.dot(xb, wb, preferred_element_type=F32)
    pre = acc_a + pltpu.roll(acc_b, ncp - 1, 0)
    hdn = _gelu(pre).astype(BF16)
    return jnp.dot(hdn, w2_ref[kind], preferred_element_type=F32)


def _compress_p_kernel(ck_ref, cv_ref, pe_ref, w1_ref, w2_ref, kc_ref, vc_ref):
    ncp = kc_ref.shape[1]

    def rows_of(ref):
        return lambda l: ref[0, pl.ds(l, ncp, stride=CMP_STRIDE), :]

    kc_ref[0] = _compress_pair(rows_of(ck_ref), ncp, pe_ref, w1_ref, w2_ref, 0).astype(BF16)
    vc_ref[0] = _compress_pair(rows_of(cv_ref), ncp, pe_ref, w1_ref, w2_ref, 1)


def _compress_prompt(c_kv, pe2, w1bd, w2bd):
    b, t, _ = c_kv.shape
    ncp = t // CMP_STRIDE
    const = lambda *shape: pl.BlockSpec(shape, lambda bi: (0,) * len(shape))
    return pl.pallas_call(
        _compress_p_kernel,
        grid=(b,),
        in_specs=[pl.BlockSpec((1, t, LANES), lambda bi: (bi, 0, 0)),
                  pl.BlockSpec((1, t, LANES), lambda bi: (bi, 0, 1)),
                  const(2, CMP_BLOCK, LANES), const(2, CMP_BLOCK, LANES, 2 * CMP_HIDDEN),
                  const(2, 2 * CMP_HIDDEN, LANES)],
        out_specs=[pl.BlockSpec((1, ncp, LANES), lambda bi: (bi, 0, 0)),
                   pl.BlockSpec((1, ncp, LANES), lambda bi: (bi, 0, 0))],
        out_shape=[jax.ShapeDtypeStruct((b, ncp, LANES), BF16), jax.ShapeDtypeStruct((b, ncp, LANES), F32)],
        compiler_params=_cparams(("parallel",)),
        name="compress_prompt",
    )(c_kv, c_kv, pe2, w1bd, w2bd)


def _nsa_p_kernel(q0_ref, q1_ref, q2_ref, ckv_ref, cwin_ref, gt_ref, t3_ref, w5_ref, bc_ref, cov_ref,
                  kcb, vc_ref, o_ref, vctb, skb, svtb, wkb, wvtb, selsc, sbuf):
    qi = pl.program_id(1)
    t = skb.shape[0]
    ncp = kcb.shape[1]
    nbs = selsc.shape[1]
    per_tile = TQ // SLC_BLOCK

    @pl.when(qi == 0)
    def _prep():
        vctb[...] = vc_ref[0].T.astype(BF16)
        skb[...] = ckv_ref[0, :, 0:LANES].astype(BF16)
        svtb[...] = ckv_ref[0, :, LANES:2 * LANES].T.astype(BF16)
        wkb[...] = cwin_ref[0, :, 0:LANES].astype(BF16)
        wvtb[...] = cwin_ref[0, :, LANES:2 * LANES].T.astype(BF16)

    q_refs = (q0_ref, q1_ref, q2_ref)
    qpos = qi * TQ + lax.broadcasted_iota(jnp.int32, (1, TQ), 1)
    ci = lax.broadcasted_iota(jnp.int32, (ncp, TQ), 0)
    cok = (qpos >= ci * CMP_STRIDE + (CMP_BLOCK - 1)) & (ci < ncp - 1)

    qbs = [_head_mask(q_refs[s // 2][0], s % 2).astype(BF16) for s in range(C_HEADS)]

    wn = WINDOW + TQ
    wstart = pl.multiple_of(jnp.maximum(qi - WINDOW // TQ, 0) * TQ, TQ)
    woff = pl.multiple_of((WINDOW // TQ - jnp.minimum(qi, WINDOW // TQ)) * TQ, TQ)
    o_win = _attend_t(
        [lax.dot_general(wkb[pl.ds(wstart, wn), :], qbs[s], _NT, preferred_element_type=F32)
         + w5_ref[s, pl.ds(woff, wn), :] for s in range(C_HEADS)],
        [wvtb[(s % 2) * HEAD_DIM:(s % 2 + 1) * HEAD_DIM, pl.ds(wstart, wn)] for s in range(C_HEADS)])

    band0 = pl.multiple_of(ncp - qi * (TQ // CMP_STRIDE), TQ // CMP_STRIDE)
    scs = [jnp.where(cok, lax.dot_general(kcb[0], qbs[s], _NT, preferred_element_type=F32)
                     + bc_ref[s, pl.ds(band0, ncp), :], NEG_INF) for s in range(C_HEADS)]
    pcs = []
    for sc in scs:
        m = jnp.max(sc, axis=0, keepdims=True)
        e = jnp.where(cok, jnp.exp(sc - m), 0.0)
        den = jnp.sum(e, axis=0, keepdims=True)
        pcs.append(e * jnp.where(den > 0.0, 1.0 / den, 0.0))
    psum = [pcs[kvh] + pcs[kvh + 2] + pcs[kvh + 4] for kvh in range(2)]
    ocmp = [jnp.dot(vctb[(s % 2) * HEAD_DIM:(s % 2 + 1) * HEAD_DIM, :], pcs[s].astype(BF16),
                    preferred_element_type=F32) for s in range(C_HEADS)]

    jj = lax.broadcasted_iota(jnp.int32, (nbs, TQ), 0)
    own = qpos // SLC_BLOCK
    forced = (jj == 0) | (jj == own) | (jj == own - 1)
    for kvh in range(2):
        imp = jnp.dot(cov_ref[...], psum[kvh], precision=HIGHEST, preferred_element_type=F32)
        v = jnp.where(jj <= own, jnp.where(forced, FORCE_SCORE, imp), NEG_INF)
        chosen = (_top_rank(v, jj) < SLC_TOPK) & (jj <= own)
        selsc[kvh] = jnp.where(chosen, 0.0, NEG_INF)

    def sel_rows(kvh, j):
        rows = [jnp.broadcast_to(selsc[kvh, pl.ds(j * per_tile + r, 1), :], (SLC_BLOCK, TQ))
                for r in range(per_tile)]
        return jnp.concatenate(rows, axis=0)

    def score_fn(s, i):
        j = qi - i
        kblk = skb[pl.ds(pl.multiple_of(j * TQ, TQ), TQ), :]
        bias = t3_ref[s, pl.ds(pl.multiple_of(jnp.minimum(i, 2) * TQ, TQ), TQ), :]
        return lax.dot_general(kblk, qbs[s], _NT, preferred_element_type=F32) + bias + sel_rows(s % 2, j)

    def vt_fn(s, i):
        kvh = s % 2
        return svtb[kvh * HEAD_DIM:(kvh + 1) * HEAD_DIM, pl.ds(pl.multiple_of((qi - i) * TQ, TQ), TQ)]

    st_slc = _flash_pipeline(qi + 1, score_fn, vt_fn, sbuf, heads=C_HEADS)

    for pair in range(C_HEADS // 2):
        outs = []
        for s in (2 * pair, 2 * pair + 1):
            o_slc = st_slc[3 * s + 2] / st_slc[3 * s + 1]
            grow = B_HEADS + C_SLOT_HEADS[s] * N_GATES
            g = [jax.nn.sigmoid(gt_ref[0, grow + n_:grow + n_ + 1, :]) for n_ in range(N_GATES)]
            outs.append(g[0] * ocmp[s] + g[1] * o_slc + g[2] * o_win[s])
        o_ref[0, :, pair * LANES:(pair + 1) * LANES] = jnp.concatenate(outs, axis=0).T


def _nsa_prompt(q_all, c_kv, c_win, small_t, tiles3, tiles5, bc, cover_t, kc, vct):
    b, t, _ = q_all.shape
    assert t >= WINDOW + TQ
    nq = t // TQ
    ncp = t // CMP_STRIDE
    nbs = t // SLC_BLOCK
    qoff = (A_HEADS + B_HEADS) // 2
    const = lambda *shape: pl.BlockSpec(shape, lambda bi, qi: (0,) * len(shape))
    qspec = lambda m: pl.BlockSpec((1, TQ, LANES), lambda bi, qi: (bi, qi, qoff + m))
    return pl.pallas_call(
        _nsa_p_kernel,
        grid=(b, nq),
        in_specs=[qspec(0), qspec(1), qspec(2),
                  pl.BlockSpec((1, t, 2 * LANES), lambda bi, qi: (bi, 0, 1)),
                  pl.BlockSpec((1, t, CWIN_COLS), lambda bi, qi: (bi, 0, 0)),
                  pl.BlockSpec((1, SMALL_COLS, TQ), lambda bi, qi: (bi, 0, qi)),
                  const(C_HEADS, 3 * TQ, TQ),
                  const(C_HEADS, WINDOW + 3 * TQ, TQ),
                  const(C_HEADS, 2 * ncp, TQ),
                  const(nbs, ncp),
                  pl.BlockSpec((1, ncp, LANES), lambda bi, qi: (bi, 0, 0)),
                  pl.BlockSpec((1, ncp, LANES), lambda bi, qi: (bi, 0, 0))],
        out_specs=pl.BlockSpec((1, TQ, C_HEADS * HEAD_DIM), lambda bi, qi: (bi, qi, 0)),
        out_shape=jax.ShapeDtypeStruct((b, t, C_HEADS * HEAD_DIM), F32),
        scratch_shapes=[pltpu.VMEM((LANES, ncp), BF16),
                        pltpu.VMEM((t, LANES), BF16), pltpu.VMEM((LANES, t), BF16),
                        pltpu.VMEM((t, LANES), BF16), pltpu.VMEM((LANES, t), BF16),
                        pltpu.VMEM((2, nbs, TQ), F32), pltpu.VMEM((C_HEADS, TQ, TQ), F32)],
        compiler_params=_cparams(("parallel", "arbitrary")),
        name="nsa_prompt",
    )(q_all, q_all, q_all, c_kv, c_win, small_t, tiles3, tiles5, bc, cover_t, kc, vct)


FF_CHUNK = 256
FFN_SLAB = 256


def _rms(x, g):
    ms = jnp.mean(x * x, axis=-1, keepdims=True)
    return (x * lax.rsqrt(ms + RMS_EPS)) * g


def _ffn_kernel(sample, blocks_per_seq, final, x_ref, ya_ref, yb_ref, yc_ref, woa_ref, wob_ref, woc_ref,
                g2_ref, gf_ref, wg_ref, wu_ref, wd_ref, cw_ref, cb_ref, *rest):
    if sample:
        p1_ref, p2_ref, o_ref, gk_ref, x1_s, h2_s, act_s = rest
    else:
        o_ref, gk_ref, x1_s, h2_s, act_s, halo_s = rest
    i = pl.program_id(0)
    c = pl.program_id(1)
    last = pl.num_programs(1) - 1
    tm = x_ref.shape[0]

    col = pl.ds(pl.multiple_of(c * FF_CHUNK, FF_CHUNK), FF_CHUNK)

    def activate(rows, g, u, gs1, gs2):
        conv = cb_ref[:, col] + cw_ref[0:1, col] * gs2
        conv = conv + cw_ref[1:2, col] * gs1
        conv = conv + cw_ref[2:3, col] * g
        act_s[rows, col] = (_gelu(conv) * u).astype(BF16)

    @pl.when(c == 0)
    def _attn_out():
        y = (jnp.dot(ya_ref[...].astype(BF16), woa_ref[...], preferred_element_type=F32)
             + jnp.dot(yb_ref[...].astype(BF16), wob_ref[...], preferred_element_type=F32)
             + jnp.dot(yc_ref[...].astype(BF16), woc_ref[...], preferred_element_type=F32))
        x1 = x_ref[...] + y
        x1_s[...] = x1
        h2_s[...] = _rms(x1, g2_ref[...]).astype(BF16)

    rs = FFN_SLAB if tm % FFN_SLAB == 0 else tm
    slabs = [slice(r0, r0 + rs) for r0 in range(0, tm, rs)]
    gs = [jnp.dot(h2_s[r, :], wg_ref[...], preferred_element_type=F32) for r in slabs]
    us = [jnp.dot(h2_s[r, :], wu_ref[...], preferred_element_type=F32) for r in slabs]
    row = lax.broadcasted_iota(jnp.int32, (rs, FF_CHUNK), 0)
    if sample:
        t = row % SUBLANES
        for r, g, u in zip(slabs, gs, us):
            gs1 = jnp.where(t >= 1, pltpu.roll(g, 1, 0), p1_ref[r, :])
            gs2 = jnp.where(t >= 2, pltpu.roll(g, 2, 0), p2_ref[r, :])
            gk_ref[r, :] = g
            activate(r, g, u, gs1, gs2)
    else:
        halo = jnp.where(i % blocks_per_seq == 0, 0.0, halo_s[c])
        for r, g, u in zip(slabs, gs, us):
            gs1 = jnp.where(row >= 1, pltpu.roll(g, 1, 0), halo[SUBLANES - 1:SUBLANES, :])
            gs2 = jnp.where(row >= 2, pltpu.roll(g, 2, 0),
                            jnp.where(row == 0, halo[SUBLANES - 2:SUBLANES - 1, :], halo[SUBLANES - 1:SUBLANES, :]))
            activate(r, g, u, gs1, gs2)
            halo = g[rs - SUBLANES:rs, :]
        halo_s[c] = halo
        gk_ref[0] = halo

    @pl.when(c == last)
    def _finish():
        x2 = x1_s[...] + jnp.dot(act_s[...], wd_ref[...], preferred_element_type=F32)
        o_ref[...] = _rms(x2, gf_ref[...]) if final else x2


def _ffn(x2d, ya, yb, yc, woa, wob, woc, g2, gf, wg, wu, wd, cw, cb, *, sample, blocks_per_seq=1, final=False,
         inj=None, tm=512):
    n = x2d.shape[0]
    tm = min(tm, n)
    nc = D_FF // FF_CHUNK
    rows = lambda w: pl.BlockSpec((tm, w), lambda i, c: (i, 0))
    const = lambda *shape: pl.BlockSpec(shape, lambda i, c: (0,) * len(shape))
    in_specs = [rows(D_MODEL), rows(ya.shape[1]), rows(yb.shape[1]), rows(yc.shape[1]),
                const(*woa.shape), const(*wob.shape), const(*woc.shape), const(1, D_MODEL), const(1, D_MODEL),
                pl.BlockSpec((D_MODEL, FF_CHUNK), lambda i, c: (0, c)),
                pl.BlockSpec((D_MODEL, FF_CHUNK), lambda i, c: (0, c)),
                const(D_FF, D_MODEL), const(CONV_W, D_FF), const(1, D_FF)]
    args = [x2d, ya, yb, yc, woa, wob, woc, g2.reshape(1, D_MODEL), gf.reshape(1, D_MODEL), wg, wu, wd, cw,
            cb.reshape(1, D_FF)]
    scratch = [pltpu.VMEM((tm, D_MODEL), F32), pltpu.VMEM((tm, D_MODEL), BF16), pltpu.VMEM((tm, D_FF), BF16)]
    if sample:
        in_specs += [pl.BlockSpec((tm, FF_CHUNK), lambda i, c: (i, c))] * 2
        args += list(inj)
        gk_spec = pl.BlockSpec((tm, FF_CHUNK), lambda i, c: (i, c))
        gk_shape = jax.ShapeDtypeStruct((n, D_FF), F32)
    else:
        scratch.append(pltpu.VMEM((nc, SUBLANES, FF_CHUNK), F32))
        gk_spec = pl.BlockSpec((1, SUBLANES, FF_CHUNK), lambda i, c: (i, 0, c))
        gk_shape = jax.ShapeDtypeStruct((n // tm, SUBLANES, D_FF), F32)
    return pl.pallas_call(
        functools.partial(_ffn_kernel, sample, blocks_per_seq, final),
        grid=(n // tm, nc),
        in_specs=in_specs,
        out_specs=[pl.BlockSpec((tm, D_MODEL), lambda i, c: (i, 0)), gk_spec],
        out_shape=[jax.ShapeDtypeStruct((n, D_MODEL), F32), gk_shape],
        scratch_shapes=scratch,
        compiler_params=_cparams(("arbitrary", "arbitrary")),
        name="ffn_sample" if sample else "ffn_prompt",
    )(*args)


N_PAGES = 16
PAST_LEN = N_PAGES * PAGE_SIZE
DEC_SEQ = SUBLANES


def _top_rank_lanes(vals, n):
    lane = lax.broadcasted_iota(jnp.int32, vals.shape, 1)
    rank = jnp.zeros(vals.shape, F32)
    for j in range(n):
        vj = vals[:, j:j + 1]
        beats = (vj > vals) | ((vj == vals) & (j < lane))
        rank = rank + beats.astype(F32)
    return rank


def _stack_heads(q, lane_heads):
    return jnp.concatenate([_head_mask(q, h) for h in lane_heads], axis=0)


def _pad_rows(x, rows=LANES):
    return jnp.concatenate([x, jnp.zeros((rows - x.shape[0], x.shape[1]), x.dtype)], axis=0)


def _softmax_pv(groups):
    ms = []
    for parts in groups:
        m = parts[0][0].max(axis=1, keepdims=True)
        for s, _, _ in parts[1:]:
            m = jnp.maximum(m, s.max(axis=1, keepdims=True))
        ms.append(m)
    ps, ls = [], []
    for parts, m in zip(groups, ms):
        p = [jnp.exp(s - m) for s, _, _ in parts]
        ls.append(sum(jnp.sum(x, axis=1, keepdims=True) for x in p))
        ps.append([x.astype(BF16) for x in p])
    outs = []
    for parts, p, l in zip(groups, ps, ls):
        o = 0.0
        for (_, v, feature_major), pb in zip(parts, p):
            if feature_major:
                o = o + lax.dot_general(pb, v, _NT, preferred_element_type=F32)
            else:
                o = o + jnp.dot(pb, v, preferred_element_type=F32)
        outs.append(o / l)
    return outs


def _lane_cat(pages, rows):
    return jnp.concatenate([pg[0, rows, :] for pg in pages], axis=1)


def _pair_out(o, r0, r1):
    lane = lax.broadcasted_iota(jnp.int32, (DEC_SEQ, LANES), 1)
    return jnp.where(lane < HEAD_DIM, o[r0:r0 + DEC_SEQ], o[r1:r1 + DEC_SEQ])


def _moba_s_kernel(pt_ref, q_ref, new_ref, bp_ref, bn_ref, *rest):
    pages = rest[:N_PAGES]
    o_ref = rest[N_PAGES]
    nblk = PAST_LEN // MOBA_BLOCK
    ppb = MOBA_BLOCK // PAGE_SIZE
    width = A_HEADS * HEAD_DIM
    q = q_ref[0]
    lane = lax.broadcasted_iota(jnp.int32, (2 * DEC_SEQ, LANES), 1)
    t_row = lax.broadcasted_iota(jnp.int32, (2 * DEC_SEQ, LANES), 0) % DEC_SEQ
    lane_sq = lax.broadcasted_iota(jnp.int32, (LANES, LANES), 1)
    groups = []
    for p in range(A_HEADS // 2):
        rows = slice(p * LANES, (p + 1) * LANES)
        cols = slice(p * LANES, (p + 1) * LANES)
        kmean_t = jnp.zeros((LANES, LANES), F32)
        for j in range(nblk):
            both = sum(pg[0, rows, :] for pg in pages[j * ppb:(j + 1) * ppb])
            kmean_t = jnp.where(lane_sq == j, jnp.sum(both, axis=1, keepdims=True) / MOBA_BLOCK, kmean_t)
        q2 = _stack_heads(q[:, cols], (0, 1))
        gate = jnp.dot(q2, kmean_t, precision=HIGHEST, preferred_element_type=F32)
        gm = jnp.where(lane < nblk, gate, NEG_INF)
        chosen = (_top_rank_lanes(gm, nblk) < MOBA_TOPK) & (lane < nblk)
        neg = jnp.where(chosen, 0.0, NEG_INF)
        q2b = q2.astype(BF16)
        ktb = _lane_cat(pages, rows).astype(BF16)
        vtb = _lane_cat(pages, slice(width + p * LANES, width + (p + 1) * LANES)).astype(BF16)
        s_past = jnp.dot(q2b, ktb, preferred_element_type=F32) + bp_ref[p]
        s_past = s_past + jnp.concatenate([jnp.broadcast_to(neg[:, j:j + 1], (2 * DEC_SEQ, MOBA_BLOCK))
                                           for j in range(nblk)], axis=1)
        knp = _pad_rows(new_ref[0, :, cols]).astype(BF16)
        vnp = _pad_rows(new_ref[0, :, width + p * LANES:width + (p + 1) * LANES]).astype(BF16)
        s_new = lax.dot_general(q2b, knp, _NT, preferred_element_type=F32) + bn_ref[p]
        s_new = jnp.where(lane <= t_row, s_new, NEG_INF)
        groups.append([(s_past, vtb, True), (s_new, vnp, False)])
    o_ref[0] = jnp.concatenate([_pair_out(o, 0, DEC_SEQ) for o in _softmax_pv(groups)], axis=1)


def _page_specs(rows):
    def spec(i):
        return pl.BlockSpec((1, rows, PAGE_SIZE), lambda b, pt, *_: (pt[b * N_PAGES + i], 0, 0))
    return [spec(i) for i in range(N_PAGES)]


def _moba_sample(pt, q_s, a_new, cache, bias_past, bias_new):
    nbt = q_s.shape[0]
    row = lambda w: pl.BlockSpec((1, DEC_SEQ, w), lambda b, pt: (b, 0, 0))
    const = lambda *shape: pl.BlockSpec(shape, lambda b, pt: (0,) * len(shape))
    gs = pltpu.PrefetchScalarGridSpec(
        num_scalar_prefetch=1, grid=(nbt,),
        in_specs=[row(Q_COLS), row(AKV_COLS), const(*bias_past.shape), const(*bias_new.shape)]
        + _page_specs(AKV_COLS),
        out_specs=row(A_HEADS * HEAD_DIM))
    return pl.pallas_call(
        _moba_s_kernel, grid_spec=gs,
        out_shape=jax.ShapeDtypeStruct((nbt, DEC_SEQ, A_HEADS * HEAD_DIM), F32),
        compiler_params=_cparams(("arbitrary",)), name="moba_sample",
    )(pt, q_s, a_new, bias_past, bias_new, *([cache] * N_PAGES))


def _split3(x):
    hi = x.astype(BF16)
    r1 = x - hi.astype(F32)
    mid = r1.astype(BF16)
    lo = (r1 - mid.astype(F32)).astype(BF16)
    return hi, mid, lo


def _fox_s_kernel(pt_ref, raw_ref, q_ref, new_ref, small_ref, cm_ref, *rest):
    pages = rest[:N_PAGES]
    lpages = rest[N_PAGES:2 * N_PAGES]
    o_ref = rest[2 * N_PAGES]
    x_s = rest[2 * N_PAGES + 1]
    b = pl.program_id(0)
    width = B_HEADS * HEAD_DIM
    qoff = A_HEADS * HEAD_DIM
    r16 = 2 * DEC_SEQ
    lane = lax.broadcasted_iota(jnp.int32, (r16, LANES), 1)
    t_row = lax.broadcasted_iota(jnp.int32, (r16, LANES), 0) % DEC_SEQ

    x_s[...] = jnp.zeros(x_s.shape, F32)
    for pg in range(N_PAGES):
        r = raw_ref[b * N_PAGES + pg] % SUBLANES
        for h in range(B_HEADS):
            x_s[h * N_PAGES + pg:h * N_PAGES + pg + 1, :] = lpages[pg][h, pl.ds(r, 1), :]
    hi, mid, lo = _split3(x_s[...])
    parts = jnp.dot(jnp.concatenate([hi, mid, lo], axis=0), cm_ref[0], preferred_element_type=F32)
    sfx = parts[0:LANES] + parts[LANES:2 * LANES] + parts[2 * LANES:3 * LANES]
    t_hi, t_mid, t_lo = _split3(sfx[:, LANES:])
    offp = jnp.dot(cm_ref[1, :, 0:LANES], jnp.concatenate([t_hi, t_mid, t_lo], axis=1),
                   preferred_element_type=F32)
    arow = sfx[:, 0:LANES] + (offp[:, 0:LANES] + offp[:, LANES:2 * LANES] + offp[:, 2 * LANES:3 * LANES])

    def decay_row(h):
        return jnp.concatenate([arow[h * N_PAGES + pg:h * N_PAGES + pg + 1, :] for pg in range(N_PAGES)], axis=1)

    lfn = small_ref[0]
    sub = lax.broadcasted_iota(jnp.int32, (DEC_SEQ, LANES), 0)
    lane8 = lax.broadcasted_iota(jnp.int32, (DEC_SEQ, LANES), 1)
    cs = lfn
    for sh in (1, 2, 4):
        cs = cs + jnp.where(sub >= sh, pltpu.roll(cs, sh, 0), 0.0)

    q = q_ref[0]
    groups = []
    for p in range(B_HEADS // 2):
        cols = slice(p * LANES, (p + 1) * LANES)
        q2b = _stack_heads(q[:, qoff + p * LANES:qoff + (p + 1) * LANES], (0, 1)).astype(BF16)
        heads = (2 * p, 2 * p + 1)
        a_rows = jnp.concatenate([jnp.broadcast_to(decay_row(h), (DEC_SEQ, PAST_LEN)) for h in heads], axis=0)
        d_col = jnp.concatenate([cs[:, h:h + 1] for h in heads], axis=0)
        d_row = jnp.concatenate(
            [jnp.broadcast_to(jnp.sum(jnp.where(sub <= lane8, jnp.broadcast_to(lfn[:, h:h + 1], (DEC_SEQ, LANES)),
                                                0.0), axis=0, keepdims=True), (DEC_SEQ, LANES))
             for h in heads], axis=0)
        ktb = _lane_cat(pages, cols).astype(BF16)
        vtb = _lane_cat(pages, slice(width + p * LANES, width + (p + 1) * LANES)).astype(BF16)
        s_past = jnp.dot(q2b, ktb, preferred_element_type=F32) + (a_rows + d_col)
        knp = _pad_rows(new_ref[0, :, cols]).astype(BF16)
        vnp = _pad_rows(new_ref[0, :, width + p * LANES:width + (p + 1) * LANES]).astype(BF16)
        s_new = lax.dot_general(q2b, knp, _NT, preferred_element_type=F32) + (d_col - d_row)
        s_new = jnp.where(lane <= t_row, s_new, NEG_INF)
        groups.append([(s_past, vtb, True), (s_new, vnp, False)])
    o_ref[0] = jnp.concatenate([_pair_out(o, 0, DEC_SEQ) for o in _softmax_pv(groups)], axis=1)


def _fox_consts():
    tok = np.arange(PAGE_SIZE)
    m = np.concatenate([tok[:, None] > tok[None, :], np.ones((PAGE_SIZE, PAGE_SIZE), bool)], axis=1)
    r = np.arange(LANES)
    u = (r[:, None] // N_PAGES == r[None, :] // N_PAGES) & (r[None, :] % N_PAGES > r[:, None] % N_PAGES)
    u = np.concatenate([u, np.zeros((LANES, LANES), bool)], axis=1)
    return jnp.asarray(np.stack([m, u]), BF16)


def _fox_sample(pt, pt_raw, q_s, b_new, small_s, cache, cache_logf, layer):
    nbt = q_s.shape[0]
    row = lambda w: pl.BlockSpec((1, DEC_SEQ, w), lambda b, pt, raw: (b, 0, 0))
    cm = _fox_consts()
    lspecs = [pl.BlockSpec((B_HEADS, SUBLANES, PAGE_SIZE),
                           functools.partial(lambda i, b, pt, raw: (layer, raw[b * N_PAGES + i] // SUBLANES, 0), i))
              for i in range(N_PAGES)]
    gs = pltpu.PrefetchScalarGridSpec(
        num_scalar_prefetch=2, grid=(nbt,),
        in_specs=[row(Q_COLS), row(BKV_COLS), row(SMALL_COLS),
                  pl.BlockSpec(cm.shape, lambda b, pt, raw: (0, 0, 0))] + _page_specs(BKV_COLS) + lspecs,
        out_specs=row(B_HEADS * HEAD_DIM),
        scratch_shapes=[pltpu.VMEM((LANES, PAGE_SIZE), F32)])
    return pl.pallas_call(
        _fox_s_kernel, grid_spec=gs,
        out_shape=jax.ShapeDtypeStruct((nbt, DEC_SEQ, B_HEADS * HEAD_DIM), F32),
        compiler_params=_cparams(("arbitrary",)), name="fox_sample",
    )(pt, pt_raw, q_s, b_new, small_s, cm, *([cache] * N_PAGES), *([cache_logf] * N_PAGES))


def _nsa_s_kernel(pt_ref, q_ref, new_ref, wnew_ref, gate_ref, win_ref, bc_ref, bp_ref, bn_ref, bw_ref,
                  cov_ref, exp_ref, pe_ref, w1_ref, w2_ref, *rest):
    pages = rest[:N_PAGES]
    o_ref = rest[N_PAGES]
    ctok = rest[N_PAGES + 1]
    ncp = PAST_LEN // CMP_STRIDE
    nbs_past = PAST_LEN // SLC_BLOCK
    qoff = (A_HEADS + B_HEADS) * HEAD_DIM
    nrow = C_HEADS * DEC_SEQ
    lane = lax.broadcasted_iota(jnp.int32, (nrow, LANES), 1)
    t_row = lax.broadcasted_iota(jnp.int32, (nrow, LANES), 0) % DEC_SEQ

    q = q_ref[0]
    q2 = jnp.concatenate([_head_mask(q[:, qoff + (s // 2) * LANES:qoff + (s // 2 + 1) * LANES], s % 2)
                          for s in range(C_HEADS)], axis=0)
    q2b = q2.astype(BF16)

    ktb = _lane_cat(pages, slice(2 * LANES, 3 * LANES)).astype(BF16)
    vtb = _lane_cat(pages, slice(3 * LANES, 4 * LANES)).astype(BF16)
    s_past = jnp.dot(q2b, ktb, preferred_element_type=F32) + bp_ref[...]
    knp = _pad_rows(new_ref[0, :, 2 * LANES:3 * LANES]).astype(BF16)
    vnp = _pad_rows(new_ref[0, :, 3 * LANES:4 * LANES]).astype(BF16)
    s_new = lax.dot_general(q2b, knp, _NT, preferred_element_type=F32) + bn_ref[...]
    wkt = win_ref[0, 0:LANES, :].astype(BF16)
    wvt = win_ref[0, LANES:2 * LANES, :].astype(BF16)
    wlane = lax.broadcasted_iota(jnp.int32, (nrow, WINDOW), 1)
    wt = lax.broadcasted_iota(jnp.int32, (nrow, WINDOW), 0) % DEC_SEQ
    s_w = jnp.dot(q2b, wkt, preferred_element_type=F32) + bw_ref[...]
    s_w = jnp.where(wlane > wt, s_w, NEG_INF)
    wkn = _pad_rows(wnew_ref[0, :, 0:LANES]).astype(BF16)
    wvn = _pad_rows(wnew_ref[0, :, LANES:2 * LANES]).astype(BF16)
    s_wn = lax.dot_general(q2b, wkn, _NT, preferred_element_type=F32) + bn_ref[...]
    s_wn = jnp.where(lane <= t_row, s_wn, NEG_INF)

    for i, pg in enumerate(pages):
        for kind in range(2):
            ctok[kind, i * PAGE_SIZE:(i + 1) * PAGE_SIZE, :] = pg[0, kind * LANES:(kind + 1) * LANES, :].T

    def rows_of(kind):
        return lambda l: ctok[kind, pl.ds(l, ncp, stride=CMP_STRIDE), :]

    kc = _compress_pair(rows_of(0), ncp, pe_ref, w1_ref, w2_ref, 0).astype(BF16)
    vc = _compress_pair(rows_of(1), ncp, pe_ref, w1_ref, w2_ref, 1).astype(BF16)

    sc = lax.dot_general(q2b, kc, _NT, preferred_element_type=F32) + bc_ref[...]
    sc = jnp.where(lane < ncp - 1, sc, NEG_INF)
    m = sc.max(axis=1, keepdims=True)
    e = jnp.exp(sc - m)
    pc = e / jnp.sum(e, axis=1, keepdims=True)
    o_cmp = jnp.dot(pc.astype(BF16), vc, preferred_element_type=F32)

    psum = [sum(pc[s * DEC_SEQ:(s + 1) * DEC_SEQ] for s in range(kvh, C_HEADS, 2)) for kvh in range(2)]
    imp = jnp.dot(jnp.concatenate(psum, axis=0), cov_ref[...], precision=HIGHEST, preferred_element_type=F32)
    lane16 = lax.broadcasted_iota(jnp.int32, (2 * DEC_SEQ, LANES), 1)
    forced = (lane16 == 0) | (lane16 == nbs_past) | (lane16 == nbs_past - 1)
    v = jnp.where(lane16 <= nbs_past, jnp.where(forced, FORCE_SCORE, imp), NEG_INF)
    chosen = (_top_rank_lanes(v, nbs_past + 1) < SLC_TOPK) & (lane16 <= nbs_past)
    chosen_f = jnp.where(chosen, 1.0, 0.0)
    mask16 = jnp.dot(chosen_f.astype(BF16), exp_ref[...], preferred_element_type=F32)
    mask_past = jnp.concatenate([mask16[(s % 2) * DEC_SEQ:(s % 2 + 1) * DEC_SEQ] for s in range(C_HEADS)], axis=0)
    mask_new = jnp.concatenate(
        [jnp.broadcast_to(chosen_f[(s % 2) * DEC_SEQ:(s % 2 + 1) * DEC_SEQ, nbs_past:nbs_past + 1],
                          (DEC_SEQ, LANES)) for s in range(C_HEADS)], axis=0)

    s_past = jnp.where(mask_past > 0.5, s_past, NEG_INF)
    s_new = jnp.where((lane <= t_row) & (mask_new > 0.5), s_new, NEG_INF)
    o_slc, o_win = _softmax_pv([[(s_past, vtb, True), (s_new, vnp, False)],
                                [(s_w, wvt, True), (s_wn, wvn, False)]])

    g = jax.nn.sigmoid(gate_ref[0])
    o = g[:, 0:1] * o_cmp + g[:, 1:2] * o_slc + g[:, 2:3] * o_win
    o_ref[0] = jnp.concatenate([_pair_out(o, 2 * m_ * DEC_SEQ, (2 * m_ + 1) * DEC_SEQ)
                                for m_ in range(C_HEADS // 2)], axis=1)


def _nsa_sample(pt, q_s, c_new, w_new, gates, win_state, layer, cache, tables, pe2, w1bd, w2bd):
    nbt = q_s.shape[0]
    bc, bp, bn, bw, cov, expand = tables
    row = lambda r, w: pl.BlockSpec((1, r, w), lambda b, pt: (b, 0, 0))
    win_spec = pl.BlockSpec((1, CWIN_COLS, WINDOW), lambda b, pt: (layer * nbt + b, 0, 0))
    const = lambda a: pl.BlockSpec(a.shape, lambda b, pt: (0,) * a.ndim)
    gs = pltpu.PrefetchScalarGridSpec(
        num_scalar_prefetch=1, grid=(nbt,),
        in_specs=[row(DEC_SEQ, Q_COLS), row(DEC_SEQ, CKV_COLS), row(DEC_SEQ, CWIN_COLS),
                  row(C_HEADS * DEC_SEQ, LANES), win_spec,
                  const(bc), const(bp), const(bn), const(bw), const(cov), const(expand),
                  const(pe2), const(w1bd), const(w2bd)] + _page_specs(CKV_COLS),
        out_specs=row(DEC_SEQ, C_HEADS * HEAD_DIM),
        scratch_shapes=[pltpu.VMEM((2, PAST_LEN, LANES), F32)])
    return pl.pallas_call(
        _nsa_s_kernel, grid_spec=gs,
        out_shape=jax.ShapeDtypeStruct((nbt, DEC_SEQ, C_HEADS * HEAD_DIM), F32),
        compiler_params=_cparams(("arbitrary",)), name="nsa_sample",
    )(pt, q_s, c_new, w_new, gates, win_state, bc, bp, bn, bw, cov, expand, pe2, w1bd, w2bd,
      *([cache] * N_PAGES))


def _sample_mix_kernel(pt_ref, raw_ref, q_ref, anew_ref, bnew_ref, small_ref, cnew_ref, wnew_ref, gate_ref, win_ref,
                       bpa_ref, bna_ref, cm_ref, bc_ref, bp_ref, bn_ref, bw_ref, cov_ref, exp_ref,
                       pe_ref, w1_ref, w2_ref, *rest):
    n = N_PAGES
    a_pages, b_pages, lf_pages, c_pages = rest[0:n], rest[n:2 * n], rest[2 * n:3 * n], rest[3 * n:4 * n]
    oa_ref, ob_ref, oc_ref, x_s, ctok = rest[4 * n:4 * n + 5]
    _moba_s_kernel(pt_ref, q_ref, anew_ref, bpa_ref, bna_ref, *a_pages, oa_ref)
    _fox_s_kernel(pt_ref, raw_ref, q_ref, bnew_ref, small_ref, cm_ref, *b_pages, *lf_pages, ob_ref, x_s)
    _nsa_s_kernel(pt_ref, q_ref, cnew_ref, wnew_ref, gate_ref, win_ref, bc_ref, bp_ref, bn_ref, bw_ref,
                  cov_ref, exp_ref, pe_ref, w1_ref, w2_ref, *c_pages, oc_ref, ctok)


def _sample_mixers(pt, pt_raw, layer, q_s, a_new, b_new, small_s, c_new, w_new, gates, win_state,
                   cache_a, cache_b, cache_lf, cache_c, moba_tabs, nsa_tabs, pe2, w1bd, w2bd):
    nbt = q_s.shape[0]
    cm = _fox_consts()
    row = lambda r, w: pl.BlockSpec((1, r, w), lambda b, *_: (b, 0, 0))
    const = lambda a: pl.BlockSpec(a.shape, lambda b, *_: (0,) * a.ndim)
    win_spec = pl.BlockSpec((1, CWIN_COLS, WINDOW), lambda b, *_: (layer * nbt + b, 0, 0))
    lspecs = [pl.BlockSpec((B_HEADS, SUBLANES, PAGE_SIZE),
                           functools.partial(lambda i, b, pt, raw: (layer, raw[b * N_PAGES + i] // SUBLANES, 0), i))
              for i in range(N_PAGES)]
    consts = list(moba_tabs) + [cm] + list(nsa_tabs) + [pe2, w1bd, w2bd]
    gs = pltpu.PrefetchScalarGridSpec(
        num_scalar_prefetch=2, grid=(nbt,),
        in_specs=[row(DEC_SEQ, Q_COLS), row(DEC_SEQ, AKV_COLS), row(DEC_SEQ, BKV_COLS), row(DEC_SEQ, SMALL_COLS),
                  row(DEC_SEQ, CKV_COLS), row(DEC_SEQ, CWIN_COLS), row(C_HEADS * DEC_SEQ, LANES), win_spec]
        + [const(a) for a in consts]
        + _page_specs(AKV_COLS) + _page_specs(BKV_COLS) + lspecs + _page_specs(CKV_COLS),
        out_specs=[row(DEC_SEQ, A_HEADS * HEAD_DIM), row(DEC_SEQ, B_HEADS * HEAD_DIM),
                   row(DEC_SEQ, C_HEADS * HEAD_DIM)],
        scratch_shapes=[pltpu.VMEM((LANES, PAGE_SIZE), F32), pltpu.VMEM((2, PAST_LEN, LANES), F32)])
    return pl.pallas_call(
        _sample_mix_kernel, grid_spec=gs,
        out_shape=[jax.ShapeDtypeStruct((nbt, DEC_SEQ, h * HEAD_DIM), F32) for h in (A_HEADS, B_HEADS, C_HEADS)],
        compiler_params=_cparams(("arbitrary",)), name="sample_mixers",
    )(pt, pt_raw, q_s, a_new, b_new, small_s, c_new, w_new, gates, win_state, *consts,
      *([cache_a] * N_PAGES), *([cache_b] * N_PAGES), *([cache_lf] * N_PAGES), *([cache_c] * N_PAGES))


def _bucket_of(dist):
    return _BUCKETS[np.clip(dist, 0, MAX_DISTANCE)].astype(np.int32)


_SAMPLE_KEY_POS = np.concatenate([
    np.arange(PAST_LEN),
    PAST_LEN + np.arange(LANES),
    PAST_LEN - WINDOW + np.arange(WINDOW),
    np.arange(PAST_LEN // CMP_STRIDE) * CMP_STRIDE + CMP_BLOCK - 1
])


def _bias_bucket_tables(t):
    k = np.arange(TQ)[:, None]
    q = np.arange(TQ)[None, :]
    ncp = t // CMP_STRIDE
    rel = np.arange(2 * ncp)[:, None] - ncp
    band = _bucket_of(q - (rel * CMP_STRIDE + CMP_BLOCK - 1))
    samp = _bucket_of(PAST_LEN + np.arange(DEC_SEQ)[:, None] - _SAMPLE_KEY_POS[None, :])
    masked = np.full((TQ, TQ), _MASKED_CODE, np.int32)
    own = np.where(q >= k, _bucket_of(q - k), masked)
    prev = _bucket_of(TQ + q - k)
    far = _bucket_of(np.full((TQ, TQ), 2 * TQ))
    far_win = np.where(k > q, far, masked)
    tiles3 = np.concatenate([own, prev, far], axis=0)
    tiles5 = np.concatenate([far_win, prev, own] + [masked] * (WINDOW // TQ), axis=0)
    return [jnp.asarray(a) for a in (tiles3, tiles5, band, samp)]


_MASKED_CODE = N_BUCKETS


def _bias_kernel(tab_ref, *refs):
    h = pl.program_id(0)
    n = len(refs) // 2
    for idx_ref, o_ref in zip(refs[:n], refs[n:]):
        idx = idx_ref[...]
        acc = jnp.full(idx.shape, NEG_INF, F32)
        for bkt in range(N_BUCKETS):
            acc = jnp.where(idx == bkt, tab_ref[bkt, h], acc)
        o_ref[0] = acc


def _bias_tables(rel_bias, t):
    idx = _bias_bucket_tables(t)
    nh = rel_bias.shape[1]
    return pl.pallas_call(
        _bias_kernel,
        grid=(nh,),
        in_specs=[pl.BlockSpec(memory_space=pltpu.SMEM)]
        + [pl.BlockSpec(a.shape, lambda h: (0, 0)) for a in idx],
        out_specs=[pl.BlockSpec((1,) + a.shape, lambda h: (h, 0, 0)) for a in idx],
        out_shape=[jax.ShapeDtypeStruct((nh,) + a.shape, F32) for a in idx],
        compiler_params=_cparams(("arbitrary",)),
        name="bias_tables",
    )(rel_bias, *idx)


def _sample_masks():
    ncp = PAST_LEN // CMP_STRIDE
    nbs = PAST_LEN // SLC_BLOCK
    ci = np.arange(ncp)[:, None]
    bj = np.arange(LANES)[None, :]
    cov = ((ci * CMP_STRIDE < bj * SLC_BLOCK + SLC_BLOCK) & (ci * CMP_STRIDE + CMP_BLOCK > bj * SLC_BLOCK)
           & (ci < ncp - 1) & (bj < nbs))
    expand = np.arange(LANES)[:, None] == (np.arange(PAST_LEN)[None, :] // SLC_BLOCK)
    return jnp.asarray(cov, F32), jnp.asarray(expand, BF16)


def _prep_compress(cmp_pe, cmp_w1, cmp_w2):
    pe2 = jnp.concatenate([cmp_pe, cmp_pe], axis=-1)
    z1 = jnp.zeros_like(cmp_w1)
    w1bd = jnp.concatenate([jnp.concatenate([cmp_w1, z1], axis=-1),
                            jnp.concatenate([z1, cmp_w1], axis=-1)], axis=-2)
    z2 = jnp.zeros_like(cmp_w2)
    w2bd = jnp.concatenate([jnp.concatenate([cmp_w2, z2], axis=-1),
                            jnp.concatenate([z2, cmp_w2], axis=-1)], axis=-2)
    return pe2, w1bd.astype(BF16), w2bd.astype(BF16)


def _cover_t(nbs, ncp):
    cstart = np.arange(ncp)[None, :] * CMP_STRIDE
    bstart = np.arange(nbs)[:, None] * SLC_BLOCK
    cov = (cstart < bstart + SLC_BLOCK) & (cstart + CMP_BLOCK > bstart) & (np.arange(ncp)[None, :] < ncp - 1)
    return jnp.asarray(cov, F32)


_SPLIT_SIZES = (256, 256, 256, 384, 384, 384, 6, 384, 128, 128, 128, 128, 128, 128, 18)


def _prep_w_in(w_in):
    offs = np.cumsum(_SPLIT_SIZES)[:-1].tolist()
    aq, ak, av, bq, bk, bv, bf, cq, cck, ccv, csk, csv, cwk, cwv, cg = jnp.split(w_in, offs, axis=-1)
    lead = cq.shape[:-1]
    cq = jnp.concatenate([cq[..., h * HEAD_DIM:(h + 1) * HEAD_DIM] for h in C_SLOT_HEADS], axis=-1)
    scale = HEAD_DIM ** -0.5
    pad = jnp.zeros(lead + (SMALL_COLS - B_HEADS - C_HEADS * N_GATES,), w_in.dtype)
    w = jnp.concatenate([aq * scale, bq * scale, cq * scale, ak, av, bk, bv,
                         cck, ccv, csk, csv, cwk, cwv, bf, cg, pad], axis=-1)
    return w.astype(BF16)


def _prep_small_bias(b_f, b_gate):
    pad = jnp.zeros(b_f.shape[:-1] + (SMALL_COLS - B_HEADS - C_HEADS * N_GATES,), F32)
    return jnp.concatenate([b_f, b_gate, pad], axis=-1)[:, None, :]


def _prep_w_out(w_out):
    wa = w_out[:, :A_HEADS * HEAD_DIM]
    wb = w_out[:, A_HEADS * HEAD_DIM:(A_HEADS + B_HEADS) * HEAD_DIM]
    wc = w_out[:, (A_HEADS + B_HEADS) * HEAD_DIM:]
    wc = jnp.concatenate([wc[:, h * HEAD_DIM:(h + 1) * HEAD_DIM] for h in C_SLOT_HEADS], axis=1)
    return wa.astype(BF16), wb.astype(BF16), wc.astype(BF16)


def kernel(x_prompt, x_sample, cache_a_kv, cache_b_kv, cache_b_logf, cache_c_kv, state_c_win, state_ffn_conv,
           page_table, rel_bias, ln1_g, w_in, b_f, b_gate, cmp_pe, cmp_w1, cmp_w2, w_out, ln2_g, w_gate, w_up,
           conv_w, conv_b, w_down, final_g):
    depth = w_in.shape[0]
    b, t, _ = x_prompt.shape
    nbt, ts, _ = x_sample.shape
    n_pool = cache_a_kv.shape[1]
    assert ts == DEC_SEQ and page_table.shape[1] == N_PAGES and cache_a_kv.shape[2] == PAGE_SIZE
    assert t % TQ == 0 and state_c_win.shape[2] == WINDOW and t >= WINDOW
    ffn_tm = 1024

    w_in_p = _prep_w_in(w_in)
    small_bias = _prep_small_bias(b_f, b_gate)
    pe2, w1bd, w2bd = _prep_compress(cmp_pe, cmp_w1, cmp_w2)
    woa, wob, woc = _prep_w_out(w_out)
    wg, wu, wd = w_gate.astype(BF16), w_up.astype(BF16), w_down.astype(BF16)

    tiles3, tiles5, band, samp = _bias_tables(rel_bias, t)
    c_slots = lambda x: jnp.stack([x[A_HEADS + h] for h in C_SLOT_HEADS], axis=0)
    tiles3_a, tiles3_c, tiles5_c, band_c = tiles3[:A_HEADS], c_slots(tiles3), c_slots(tiles5), c_slots(band)
    cover_t = _cover_t(t // SLC_BLOCK, t // CMP_STRIDE)
    o1, o2, o3 = PAST_LEN, PAST_LEN + LANES, PAST_LEN + LANES + WINDOW
    samp_a = samp[:A_HEADS].reshape(A_HEADS // 2, 2 * DEC_SEQ, -1)
    moba_tabs = (samp_a[..., :o1], samp_a[..., o1:o2])
    samp_c = c_slots(samp).reshape(C_HEADS * DEC_SEQ, -1)
    nsa_tabs = (samp_c[:, o3:], samp_c[:, :o1], samp_c[:, o1:o2], samp_c[:, o2:o3]) + _sample_masks()

    fm = lambda c: jnp.transpose(c, (0, 1, 3, 4, 5, 2))
    cache_a = fm(cache_a_kv).reshape(depth * n_pool, AKV_COLS, PAGE_SIZE)
    cache_b = fm(cache_b_kv).reshape(depth * n_pool, BKV_COLS, PAGE_SIZE)
    cache_c = fm(cache_c_kv).reshape(depth * n_pool, CKV_COLS, PAGE_SIZE)
    cache_lf = jnp.transpose(cache_b_logf, (0, 3, 1, 2)).reshape(depth * B_HEADS, n_pool, PAGE_SIZE)
    win_all = fm(state_c_win).reshape(depth * nbt, CWIN_COLS, WINDOW)
    pt_flat = page_table.reshape(-1).astype(jnp.int32)

    xp = x_prompt.reshape(b * t, D_MODEL)
    xs = x_sample.reshape(nbt * ts, D_MODEL)
    outs_p = [[] for _ in range(6)]
    outs_s = [[] for _ in range(6)]
    for l in range(depth):
        final = l == depth - 1
        q, akv, bkv, ckv, cwin, small = [o.reshape(b, t, -1) for o in
                                         _in_proj(xp, ln1_g[l], w_in_p[l], small_bias[l])]
        ya = _moba_prompt(q, akv, tiles3_a)
        yb = _fox_prompt(q, bkv, small)
        kc, vc = _compress_prompt(ckv, pe2[l], w1bd[l], w2bd[l])
        yc = _nsa_prompt(q, ckv, cwin, jnp.swapaxes(small, 1, 2), tiles3_c, tiles5_c, band_c, cover_t, kc, vc)
        xp, gk = _ffn(xp, ya.reshape(b * t, -1), yb.reshape(b * t, -1), yc.reshape(b * t, -1),
                      woa[l], wob[l], woc[l], ln2_g[l], final_g, wg[l], wu[l], wd[l], conv_w[l], conv_b[l],
                      sample=False, blocks_per_seq=t // ffn_tm, final=final, tm=ffn_tm)
        outs_p[0].append(akv.reshape(b, t, 2, A_HEADS, HEAD_DIM))
        outs_p[1].append(bkv.reshape(b, t, 2, B_HEADS, HEAD_DIM))
        outs_p[2].append(small[..., :B_HEADS])
        outs_p[3].append(ckv.reshape(b, t, 4, C_KV_HEADS, HEAD_DIM))
        outs_p[4].append(cwin[:, t - WINDOW:].reshape(b, WINDOW, 2, C_KV_HEADS, HEAD_DIM))
        outs_p[5].append(gk.reshape(b, t // ffn_tm, SUBLANES, D_FF)[:, -1, SUBLANES - (CONV_W - 1):])
        q, akv, bkv, ckv, cwin, small = [o.reshape(nbt, ts, -1) for o in
                                         _in_proj(xs, ln1_g[l], w_in_p[l], small_bias[l])]
        pt = pt_flat + l * n_pool
        gl = small[..., B_HEADS:B_HEADS + C_HEADS * N_GATES].reshape(nbt, ts, C_HEADS, N_GATES)
        gl = jnp.concatenate([gl[:, :, h] for h in C_SLOT_HEADS], axis=1)
        gl = jnp.pad(gl, ((0, 0), (0, 0), (0, LANES - N_GATES)))
        ya, yb, yc = _sample_mixers(pt, pt_flat, l, q, akv, bkv, small, ckv, cwin, gl, win_all,
                                    cache_a, cache_b, cache_lf, cache_c, moba_tabs, nsa_tabs,
                                    pe2[l], w1bd[l], w2bd[l])
        prev = state_ffn_conv[l]
        zero = jnp.zeros((nbt, ts - 2, D_FF), F32)
        p1 = jnp.concatenate([prev[:, 1:2], zero, zero[:, :1]], axis=1).reshape(nbt * ts, D_FF)
        p2 = jnp.concatenate([prev, zero], axis=1).reshape(nbt * ts, D_FF)
        xs, g_s = _ffn(xs, ya.reshape(nbt * ts, -1), yb.reshape(nbt * ts, -1), yc.reshape(nbt * ts, -1),
                       woa[l], wob[l], woc[l], ln2_g[l], final_g, wg[l], wu[l], wd[l], conv_w[l], conv_b[l],
                       sample=True, final=final, inj=(p1, p2), tm=ffn_tm)
        outs_s[0].append(akv.reshape(nbt, ts, 2, A_HEADS, HEAD_DIM))
        outs_s[1].append(bkv.reshape(nbt, ts, 2, B_HEADS, HEAD_DIM))
        outs_s[2].append(small[..., :B_HEADS])
        outs_s[3].append(ckv.reshape(nbt, ts, 4, C_KV_HEADS, HEAD_DIM))
        outs_s[4].append(jnp.concatenate([state_c_win[l][:, ts:],
                                          cwin.reshape(nbt, ts, 2, C_KV_HEADS, HEAD_DIM)], axis=1))
        outs_s[5].append(g_s.reshape(nbt, ts, D_FF)[:, ts - (CONV_W - 1):])
    st = lambda rows: jnp.stack(rows, axis=0)
    return (xp.reshape(b, t, D_MODEL), xs.reshape(nbt, ts, D_MODEL),
            *[st(r) for r in outs_p], *[st(r) for r in outs_s])
```
